```python
import math
import jax, jax.numpy as jnp
from jax import lax
import numpy as np

D_MODEL = 1024
BATCH = 1
SEQ = 16384
DEPTH = 4

HEAD_DIM = 64
RET_HEADS = 4
RET_DK = 64
RET_DV = 128
DIL_HEADS = 4
SB_HEADS = 4
MIX_WIDTH = RET_HEADS * RET_DV + DIL_HEADS * HEAD_DIM + SB_HEADS * HEAD_DIM
IN_SIZES = (RET_HEADS * RET_DK, RET_HEADS * RET_DK, RET_HEADS * RET_DV, RET_HEADS * RET_DV,
            DIL_HEADS * HEAD_DIM, DIL_HEADS * HEAD_DIM, DIL_HEADS * HEAD_DIM,
            SB_HEADS * HEAD_DIM, SB_HEADS * HEAD_DIM, SB_HEADS * HEAD_DIM)
IN_WIDTH = sum(IN_SIZES)
D_FF = 2816
BLOCK = 128
RET_CHUNK = 128
WINDOWS = (128, 512, 2048)
DILATIONS = (1, 4, 16)
ROPE_THETA = 10000.0
ALPHA = (2.0 * DEPTH) ** 0.25
BETA = (8.0 * DEPTH) ** -0.25
LN_EPS = 1e-5
GN_EPS = 1e-6
FFN_RES = 0.5
N_MOD = 9

kernel_name = 'hybrid_retention_dilated_stickbreak_macaron'


def _layer_norm(x, gain, bias):
    xf = x.astype(jnp.float32)
    mu = jnp.mean(xf, -1, keepdims=True)
    var = jnp.mean(jnp.square(xf - mu), -1, keepdims=True)
    y = (xf - mu) * lax.rsqrt(var + LN_EPS)
    return (y * gain.astype(jnp.float32) + bias.astype(jnp.float32)).astype(x.dtype)


def _modulate(x, shift, scale):
    return x * (1.0 + scale) + shift


def _post_norm(x, y, gate, res_w, gain, bias):
    return _layer_norm(ALPHA * x + res_w * (1.0 + gate) * y, gain, bias)


def _swiglu(h, w_gate, w_up, w_down):
    return (jax.nn.silu(h @ w_gate) * (h @ w_up)) @ w_down


def _split_heads(x, n_heads):
    b, s, _ = x.shape
    return x.reshape(b, s, n_heads, -1).transpose(0, 2, 1, 3)


def _merge_heads(x):
    b, h, s, d = x.shape
    return x.transpose(0, 2, 1, 3).reshape(b, s, h * d)


def _rotate(x, inv_freq):
    s, d = x.shape[2], x.shape[3]
    ang = jnp.arange(s, dtype=jnp.float32)[:, None] * inv_freq[None, :]
    cos, sin = jnp.cos(ang), jnp.sin(ang)
    xf = x.astype(jnp.float32)
    x1, x2 = xf[..., : d // 2], xf[..., d // 2:]
    return jnp.concatenate([x1 * cos - x2 * sin, x1 * sin + x2 * cos], -1).astype(x.dtype)


def _retention(q, k, v, g):
    b, h, s, dk = q.shape
    dv = v.shape[-1]
    c = RET_CHUNK
    n = s // c
    log_gamma = jnp.log1p(-jnp.exp2(-5.0 - jnp.arange(h, dtype=jnp.float32)))
    i = jnp.arange(c, dtype=jnp.float32)
    diff = i[:, None] - i[None, :]
    decay_in = jnp.where(diff >= 0, jnp.exp(log_gamma[:, None, None] * jnp.maximum(diff, 0.0)), 0.0)
    xi = jnp.exp(log_gamma[:, None] * (i + 1.0))
    zeta = jnp.exp(log_gamma[:, None] * (c - 1.0 - i))
    gamma_c = jnp.exp(log_gamma * c)
    qc = q.astype(jnp.float32).reshape(b, h, n, c, dk)
    kc = (k.astype(jnp.float32) * dk ** -0.5).reshape(b, h, n, c, dk)
    vc = v.astype(jnp.float32).reshape(b, h, n, c, dv)
    scores = jnp.einsum('bhncd,bhnmd->bhncm', qc, kc) * decay_in[None, :, None]
    o_inner = jnp.einsum('bhncm,bhnme->bhnce', scores, vc)
    kv = jnp.einsum('bhncd,bhnce->bhnde', kc * zeta[None, :, None, :, None], vc)

    def step(state, kv_n):
        return state * gamma_c[None, :, None, None] + kv_n, state

    init = jnp.zeros((b, h, dk, dv), jnp.float32)
    _, prev = lax.scan(step, init, jnp.moveaxis(kv, 2, 0))
    prev = jnp.moveaxis(prev, 0, 2)
    o_cross = jnp.einsum('bhncd,bhnde->bhnce', qc, prev) * xi[None, :, None, :, None]
    o = (o_inner + o_cross).reshape(b, h, s, dv)
    mu = jnp.mean(o, -1, keepdims=True)
    var = jnp.mean(jnp.square(o - mu), -1, keepdims=True)
    o = (o - mu) * lax.rsqrt(var + GN_EPS)
    return jax.nn.silu(g.astype(jnp.float32)) * _merge_heads(o)


def _dilated_attention(q, k, v):
    b, h, s, d = q.shape
    nb = s // BLOCK
    n_keys = max(w // dl for w, dl in zip(WINDOWS, DILATIONS)) + 1
    dist = jnp.array(DILATIONS, jnp.int32)[:, None] * jnp.arange(n_keys, dtype=jnp.int32)[None, :]
    in_win = dist <= jnp.array(WINDOWS, jnp.int32)[:, None]
    qb = (q.astype(jnp.float32) * d ** -0.5).reshape(b, h, nb, BLOCK, d).transpose(2, 0, 1, 3, 4)
    kf = k.astype(jnp.float32)
    vf = v.astype(jnp.float32)

    def block(args):
        q_blk, start = args
        t = start + jnp.arange(BLOCK, dtype=jnp.int32)
        idx = t[:, None, None] - dist[None]
        valid = (idx >= 0) & in_win[None]
        idx = jnp.maximum(idx, 0)
        k_g = jnp.take(kf, idx, axis=2)
        v_g = jnp.take(vf, idx, axis=2)
        sc = jnp.einsum('bhqd,bhqpjd->bhqpj', q_blk, k_g)
        sc = jnp.where(valid, sc, -jnp.inf)
        m = jnp.max(sc, -1)
        e = jnp.exp(sc - m[..., None])
        den = jnp.sum(e, -1)
        o_p = jnp.einsum('bhqpj,bhqpjd->bhqpd', e, v_g) / den[..., None]
        wgt = den * jnp.exp(m - jnp.max(m, -1, keepdims=True))
        wgt = wgt / jnp.sum(wgt, -1, keepdims=True)
        return jnp.einsum('bhqp,bhqpd->bhqd', wgt, o_p)

    out = lax.map(block, (qb, jnp.arange(nb, dtype=jnp.int32) * BLOCK))
    return out.transpose(1, 2, 0, 3, 4).reshape(b, h, s, d)


def _stick_breaking(q, k, v):
    b, h, s, d = q.shape
    nb = s // BLOCK
    qb = (q.astype(jnp.float32) * d ** -0.5).reshape(b, h, nb, BLOCK, d).transpose(2, 0, 1, 3, 4)
    kf = k.astype(jnp.float32)
    vf = v.astype(jnp.float32)
    key_pos = jnp.arange(s, dtype=jnp.int32)

    def block(args):
        q_blk, start = args
        t = start + jnp.arange(BLOCK, dtype=jnp.int32)
        z = jnp.einsum('bhqd,bhsd->bhqs', q_blk, kf)
        causal = key_pos[None, :] < t[:, None]
        log_beta = jax.nn.log_sigmoid(z)
        log_keep = jnp.where(causal, jax.nn.log_sigmoid(-z), 0.0)
        between = lax.cumsum(log_keep, axis=3, reverse=True) - log_keep
        a = jnp.where(causal, jnp.exp(log_beta + between), 0.0)
        return jnp.einsum('bhqs,bhsd->bhqd', a, vf)

    out = lax.map(block, (qb, jnp.arange(nb, dtype=jnp.int32) * BLOCK))
    return out.transpose(1, 2, 0, 3, 4).reshape(b, h, s, d)


def _token_mixer(h, w_in, w_out):
    proj = h @ w_in
    points, acc = [], 0
    for size in IN_SIZES[:-1]:
        acc += size
        points.append(acc)
    rq, rk, rv, rg, dq, dk, dv, sq, sk, sv = jnp.split(proj, points, axis=-1)
    ret_freq = 1.0 / (ROPE_THETA ** jnp.linspace(0.0, 1.0, RET_DK // 2, dtype=jnp.float32))
    rope_freq = 1.0 / (ROPE_THETA ** (jnp.arange(0, HEAD_DIM, 2, dtype=jnp.float32) / HEAD_DIM))
    y_ret = _retention(_rotate(_split_heads(rq, RET_HEADS), ret_freq),
                       _rotate(_split_heads(rk, RET_HEADS), ret_freq),
                       _split_heads(rv, RET_HEADS), rg)
    y_dil = _merge_heads(_dilated_attention(_rotate(_split_heads(dq, DIL_HEADS), rope_freq),
                                            _rotate(_split_heads(dk, DIL_HEADS), rope_freq),
                                            _split_heads(dv, DIL_HEADS)))
    y_sb = _merge_heads(_stick_breaking(_split_heads(sq, SB_HEADS), _split_heads(sk, SB_HEADS),
                                        _split_heads(sv, SB_HEADS)))
    y = jnp.concatenate([y_ret, y_dil, y_sb], -1).astype(h.dtype)
    return y @ w_out


def setup_inputs(seed: int = 0) -> dict:
    key = jax.random.key(seed)
    ks = jax.random.split(key, 16)
    f32 = jnp.float32
    nrm = lambda k, shape, scale: jax.random.normal(k, shape, f32) * scale
    return {
        'x': nrm(ks[0], (BATCH, SEQ, D_MODEL), 1.0),
        'c': nrm(ks[1], (BATCH, D_MODEL), 1.0),
        'w_ada': nrm(ks[2], (DEPTH, D_MODEL, N_MOD * D_MODEL), 0.3 * D_MODEL ** -0.5),
        'b_ada': nrm(ks[3], (DEPTH, N_MOD * D_MODEL), 0.01),
        'ln_gain': 1.0 + nrm(ks[4], (DEPTH, 3, D_MODEL), 0.01),
        'ln_bias': nrm(ks[5], (DEPTH, 3, D_MODEL), 0.01),
        'ffn1_w_gate': nrm(ks[6], (DEPTH, D_MODEL, D_FF), D_MODEL ** -0.5),
        'ffn1_w_up': nrm(ks[7], (DEPTH, D_MODEL, D_FF), D_MODEL ** -0.5),
        'ffn1_w_down': nrm(ks[8], (DEPTH, D_FF, D_MODEL), BETA * D_FF ** -0.5),
        'w_in': nrm(ks[9], (DEPTH, D_MODEL, IN_WIDTH), D_MODEL ** -0.5),
        'w_out': nrm(ks[10], (DEPTH, MIX_WIDTH, D_MODEL), BETA * MIX_WIDTH ** -0.5),
        'ffn2_w_gate': nrm(ks[11], (DEPTH, D_MODEL, D_FF), D_MODEL ** -0.5),
        'ffn2_w_up': nrm(ks[12], (DEPTH, D_MODEL, D_FF), D_MODEL ** -0.5),
        'ffn2_w_down': nrm(ks[13], (DEPTH, D_FF, D_MODEL), BETA * D_FF ** -0.5),
    }


def reference(x, c, w_ada, b_ada, ln_gain, ln_bias, ffn1_w_gate, ffn1_w_up, ffn1_w_down,
              w_in, w_out, ffn2_w_gate, ffn2_w_up, ffn2_w_down):
    cond = jax.nn.silu(c)
    for l in range(DEPTH):
        mod = cond @ w_ada[l] + b_ada[l]
        sh1, sc1, g1, sh2, sc2, g2, sh3, sc3, g3 = jnp.split(mod[:, None, :].astype(x.dtype), N_MOD, axis=-1)
        y = _swiglu(_modulate(x, sh1, sc1), ffn1_w_gate[l], ffn1_w_up[l], ffn1_w_down[l])
        x = _post_norm(x, y, g1, FFN_RES, ln_gain[l, 0], ln_bias[l, 0])
        y = _token_mixer(_modulate(x, sh2, sc2), w_in[l], w_out[l])
        x = _post_norm(x, y, g2, 1.0, ln_gain[l, 1], ln_bias[l, 1])
        y = _swiglu(_modulate(x, sh3, sc3), ffn2_w_gate[l], ffn2_w_up[l], ffn2_w_down[l])
        x = _post_norm(x, y, g3, FFN_RES, ln_gain[l, 2], ln_bias[l, 2])
    return x
```

```python
import functools
import math

import jax
import jax.numpy as jnp
from jax import lax
from jax.experimental import pallas as pl
from jax.experimental.pallas import tpu as pltpu

HEAD_DIM = 64
N_HEADS = 4
RET_DV = 128
GROUP_W = N_HEADS * HEAD_DIM
RET_VW = N_HEADS * RET_DV
WINDOWS = (128, 512, 2048)
DILATIONS = (1, 4, 16)
ROPE_THETA = 10000.0
LN_EPS = 1e-5
GN_EPS = 1e-6
FFN_RES = 0.5
N_MOD = 9
LOG_GAMMA = tuple(math.log1p(-(2.0 ** (-5 - h))) for h in range(N_HEADS))

V7X_LANES = 128
V7X_VMEM_LIMIT_BYTES = 56 * 1024 * 1024
ROW_TILE = 512
FFN_CHUNKS = 2
RET_BLOCK = 256
DIL_TILE = 128
SB_TILE = 256
MASKED = -1e30

F32 = jnp.float32
BF16 = jnp.bfloat16


def _params(*sem):
    return pltpu.CompilerParams(dimension_semantics=sem, vmem_limit_bytes=V7X_VMEM_LIMIT_BYTES)


def _resident(shape):
    return pl.BlockSpec(shape, lambda *_: (0,) * len(shape), pipeline_mode=pl.Buffered(1))


def _dot(a, b):
    return jnp.dot(a, b, preferred_element_type=F32)


def _dot_nt(a, b):
    return lax.dot_general(a, b, (((1,), (1,)), ((), ())), preferred_element_type=F32)


def _silu(v):
    return v * jax.nn.sigmoid(v)


def _post_norm(x, y, gate, res_w, gain, bias, alpha):
    z = alpha * x + (res_w * (1.0 + gate)) * y
    mu = jnp.mean(z, axis=-1, keepdims=True)
    zc = z - mu
    var = jnp.mean(zc * zc, axis=-1, keepdims=True)
    return zc * lax.rsqrt(var + LN_EPS) * gain + bias


def _mod_kernel(c_ref, w_ref, b_ref, o_ref):
    c = c_ref[...]
    o_ref[...] = jnp.sum(_silu(c) * w_ref[...], axis=0, keepdims=True) + b_ref[...]


def _modulation(c, w_ada, b_ada):
    depth, d, n = w_ada.shape
    tn = d
    out = pl.pallas_call(
        _mod_kernel,
        out_shape=jax.ShapeDtypeStruct((depth, 1, n), F32),
        grid=(depth, n // tn),
        in_specs=[
            pl.BlockSpec((d, 1), lambda l, j: (0, 0)),
            pl.BlockSpec((None, d, tn), lambda l, j: (l, 0, j)),
            pl.BlockSpec((None, 1, tn), lambda l, j: (l, 0, j)),
        ],
        out_specs=pl.BlockSpec((None, 1, tn), lambda l, j: (l, 0, j)),
        compiler_params=_params("parallel", "parallel"),
        name="adaln_mod",
    )(c.reshape(d, 1), w_ada, b_ada.reshape(depth, 1, n))
    return out.reshape(depth, N_MOD, d)


def _ffn_kernel(x_ref, mod_ref, ln_ref, wg_ref, wu_ref, wd_ref, o_ref, *, alpha):
    x = x_ref[...]
    shift, scale, gate = mod_ref[0:1, :], mod_ref[1:2, :], mod_ref[2:3, :]
    h = (x * (1.0 + scale) + shift).astype(BF16)
    d_ff = wg_ref.shape[1]
    fc = d_ff // FFN_CHUNKS
    y = jnp.zeros(x.shape, F32)
    for ci in range(FFN_CHUNKS):
        cols = slice(ci * fc, (ci + 1) * fc)
        act = _silu(_dot(h, wg_ref[:, cols])) * _dot(h, wu_ref[:, cols])
        y = y + _dot(act.astype(BF16), wd_ref[cols, :])
    o_ref[...] = _post_norm(x, y, gate, FFN_RES, ln_ref[0:1, :], ln_ref[1:2, :], alpha)


def _ffn_sublayer(x, mod3, ln2, wg, wu, wd, alpha):
    s, d = x.shape
    d_ff = wg.shape[1]
    assert d_ff % (FFN_CHUNKS * V7X_LANES) == 0 and s % ROW_TILE == 0
    row = pl.BlockSpec((ROW_TILE, d), lambda i: (i, 0))
    return pl.pallas_call(
        functools.partial(_ffn_kernel, alpha=alpha),
        out_shape=jax.ShapeDtypeStruct((s, d), F32),
        grid=(s // ROW_TILE,),
        in_specs=[row, _resident((3, d)), _resident((2, d)),
                  _resident((d, d_ff)), _resident((d, d_ff)), _resident((d_ff, d))],
        out_specs=row,
        compiler_params=_params("parallel"),
        name="ffn_postnorm",
    )(x, mod3, ln2, wg, wu, wd)


def _rope(v, cos, sin):
    lane = lax.broadcasted_iota(jnp.int32, cos.shape, 1)
    first_half = (lane & (HEAD_DIM - 1)) < (HEAD_DIM // 2)
    halves = []
    for hf in range(GROUP_W // V7X_LANES):
        vh = v[:, hf * V7X_LANES:(hf + 1) * V7X_LANES]
        partner = jnp.where(first_half,
                            pltpu.roll(vh, V7X_LANES - HEAD_DIM // 2, 1),
                            pltpu.roll(vh, HEAD_DIM // 2, 1))
        halves.append(vh * cos + partner * sin)
    return jnp.concatenate(halves, axis=1)


def _inproj_kernel(x_ref, mod_ref, w_ref, cr_ref, sr_ref, cd_ref, sd_ref,
                   rq_ref, rk_ref, rv_ref, rg_ref, dq_ref, dk_ref, dv_ref, sq_ref, sk_ref, sv_ref):
    shift, scale = mod_ref[0:1, :], mod_ref[1:2, :]
    h = (x_ref[...] * (1.0 + scale) + shift).astype(BF16)
    qk_scale = HEAD_DIM ** -0.5
    g, vw = GROUP_W, RET_VW

    def proj(start, width):
        return _dot(h, w_ref[:, start:start + width])

    cr, sr, cd, sd = cr_ref[...], sr_ref[...], cd_ref[...], sd_ref[...]
    rq_ref[...] = _rope(proj(0, g), cr, sr).astype(BF16)
    rk_ref[...] = (_rope(proj(g, g), cr, sr) * qk_scale).astype(BF16)
    rv_ref[...] = proj(2 * g, vw).astype(BF16)
    rg_ref[...] = proj(2 * g + vw, vw)
    base = 2 * g + 2 * vw
    dq_ref[...] = (_rope(proj(base, g), cd, sd) * qk_scale).astype(BF16)
    dk_ref[...] = _rope(proj(base + g, g), cd, sd).astype(BF16)
    dv_ref[...] = proj(base + 2 * g, g).astype(BF16)
    sq_ref[...] = (proj(base + 3 * g, g) * qk_scale).astype(BF16)
    sk_ref[...] = proj(base + 4 * g, g).astype(BF16)
    sv_ref[...] = proj(base + 5 * g, g).astype(BF16)


def _in_projection(x, mod3, w_in, tables):
    s, d = x.shape
    n = w_in.shape[1]
    g, vw = GROUP_W, RET_VW
    assert n == 8 * g + 2 * vw

    def rows(width):
        return pl.BlockSpec((ROW_TILE, width), lambda i: (i, 0))

    widths = (g, g, vw, vw, g, g, g, g, g, g)
    dtypes = (BF16, BF16, BF16, F32, BF16, BF16, BF16, BF16, BF16, BF16)
    return pl.pallas_call(
        _inproj_kernel,
        out_shape=[jax.ShapeDtypeStruct((s, w), t) for w, t in zip(widths, dtypes)],
        grid=(s // ROW_TILE,),
        in_specs=[rows(d), _resident((3, d)), _resident((d, n))] + [rows(V7X_LANES)] * 4,
        out_specs=[rows(w) for w in widths],
        compiler_params=_params("parallel"),
        name="mixer_inproj",
    )(x, mod3, w_in, *tables)


def _rope_table(inv_freq, s):
    ang = jnp.arange(s, dtype=F32)[:, None] * inv_freq[None, :]
    cos, sin = jnp.cos(ang), jnp.sin(ang)
    reps = V7X_LANES // HEAD_DIM
    return (jnp.tile(jnp.concatenate([cos, cos], 1), (1, reps)),
            jnp.tile(jnp.concatenate([-sin, sin], 1), (1, reps)))


def _head_of(idx, width):
    assert width & (width - 1) == 0
    return lax.shift_right_logical(idx, width.bit_length() - 1)


def _mod_pow2(idx, width):
    assert width & (width - 1) == 0
    return idx & (width - 1)


def _log_gamma_of(head):
    lg = jnp.full(head.shape, LOG_GAMMA[N_HEADS - 1], F32)
    for hh in range(N_HEADS - 2, -1, -1):
        lg = jnp.where(head == hh, LOG_GAMMA[hh], lg)
    return lg


def _stack_heads(q, rows):
    r = lax.broadcasted_iota(jnp.int32, (N_HEADS * rows, GROUP_W), 0)
    l = lax.broadcasted_iota(jnp.int32, (N_HEADS * rows, GROUP_W), 1)
    keep = _head_of(r, rows) == _head_of(l, HEAD_DIM)
    qf = q.astype(F32)
    return jnp.where(keep, jnp.concatenate([qf] * N_HEADS, axis=0), 0.0).astype(q.dtype)


def _merge_heads(stacked, rows):
    l = lax.broadcasted_iota(jnp.int32, (rows, GROUP_W), 1)
    head = _head_of(l, HEAD_DIM)
    out = jnp.zeros((rows, GROUP_W), stacked.dtype)
    for hh in range(N_HEADS):
        out = jnp.where(head == hh, stacked[hh * rows:(hh + 1) * rows, :], out)
    return out


def _ret_kernel(q_ref, k_ref, v_ref, g_ref, o_ref, state_ref, decay_ref, xi_ref, zeta_ref, gc_ref):
    c = RET_BLOCK
    rows = N_HEADS * c

    @pl.when(pl.program_id(0) == 0)
    def _init():
        state_ref[...] = jnp.zeros(state_ref.shape, F32)
        r = lax.broadcasted_iota(jnp.int32, (rows, c), 0)
        j = lax.broadcasted_iota(jnp.int32, (rows, c), 1)
        diff = (_mod_pow2(r, c) - j).astype(F32)
        lg = _log_gamma_of(_head_of(r, c))
        decay_ref[...] = jnp.where(diff >= 0, jnp.exp(lg * jnp.maximum(diff, 0.0)), 0.0)
        r = lax.broadcasted_iota(jnp.int32, (rows, RET_DV), 0)
        xi_ref[...] = jnp.exp(_log_gamma_of(_head_of(r, c)) * (_mod_pow2(r, c).astype(F32) + 1.0))
        i = lax.broadcasted_iota(jnp.int32, (c, GROUP_W), 0)
        l = lax.broadcasted_iota(jnp.int32, (c, GROUP_W), 1)
        zeta_ref[...] = jnp.exp(_log_gamma_of(_head_of(l, HEAD_DIM)) * (c - 1.0 - i.astype(F32)))
        r = lax.broadcasted_iota(jnp.int32, (GROUP_W, RET_DV), 0)
        gc_ref[...] = jnp.exp(_log_gamma_of(_head_of(r, HEAD_DIM)) * float(c))

    q, k, v = q_ref[...], k_ref[...], v_ref[...]
    qs = _stack_heads(q, c)
    scores = (_dot_nt(qs, k) * decay_ref[...]).astype(BF16)
    inner = jnp.concatenate(
        [_dot(scores[hh * c:(hh + 1) * c, :], v[:, hh * RET_DV:(hh + 1) * RET_DV]) for hh in range(N_HEADS)],
        axis=0)
    state = state_ref[...]
    o = inner + _dot(qs, state.astype(BF16)) * xi_ref[...]
    mu = jnp.mean(o, axis=-1, keepdims=True)
    oc = o - mu
    var = jnp.mean(oc * oc, axis=-1, keepdims=True)
    on = oc * lax.rsqrt(var + GN_EPS)
    for hh in range(N_HEADS):
        lanes = slice(hh * RET_DV, (hh + 1) * RET_DV)
        o_ref[:, lanes] = (_silu(g_ref[:, lanes]) * on[hh * c:(hh + 1) * c, :]).astype(o_ref.dtype)
    kz_t = (k.astype(F32) * zeta_ref[...]).T.astype(BF16)
    kv = _dot(kz_t, v)
    gc = gc_ref[...]
    for hh in range(N_HEADS):
        rws = slice(hh * HEAD_DIM, (hh + 1) * HEAD_DIM)
        state_ref[rws, :] = state[rws, :] * gc[rws, :] + kv[rws, hh * RET_DV:(hh + 1) * RET_DV]


def _retention(rq, rk, rv, rg):
    s = rq.shape[0]
    c = RET_BLOCK
    assert s % c == 0

    def rows(width):
        return pl.BlockSpec((c, width), lambda i: (i, 0))

    return pl.pallas_call(
        _ret_kernel,
        out_shape=jax.ShapeDtypeStruct((s, RET_VW), BF16),
        grid=(s // c,),
        in_specs=[rows(GROUP_W), rows(GROUP_W), rows(RET_VW), rows(RET_VW)],
        out_specs=rows(RET_VW),
        scratch_shapes=[pltpu.VMEM((GROUP_W, RET_DV), F32), pltpu.VMEM((N_HEADS * c, c), F32),
                        pltpu.VMEM((N_HEADS * c, RET_DV), F32), pltpu.VMEM((c, GROUP_W), F32),
                        pltpu.VMEM((GROUP_W, RET_DV), F32)],
        compiler_params=_params("arbitrary"),
        name="retention",
    )(rq, rk, rv, rg)


def _dil_kernel(q_ref, k_ref, v_ref, o_ref, m_ref, l_ref, acc_ref):
    t = DIL_TILE
    rows = N_HEADS * t
    qi = pl.program_id(0)
    q0 = qi * t
    qs = _stack_heads(q_ref[...], t)
    m_ref[...] = jnp.full(m_ref.shape, MASKED, F32)
    l_ref[...] = jnp.zeros(l_ref.shape, F32)
    acc_ref[...] = jnp.zeros(acc_ref.shape, F32)
    r = lax.broadcasted_iota(jnp.int32, (rows, t), 0)
    col = lax.broadcasted_iota(jnp.int32, (rows, t), 1)
    rc = _mod_pow2(r, t) - col

    for dil, win in zip(DILATIONS, WINDOWS):
        on_grid = jnp.where((rc & (dil - 1)) == 0, 0.0, MASKED)
        n_tiles = jnp.minimum(win // t, qi) + 1

        def tile(jj, carry, on_grid=on_grid, win=win):
            back = jj * t
            k0 = pl.multiple_of(q0 - back, t)
            z = _dot_nt(qs, k_ref[pl.ds(k0, t), :])
            dist = rc + back
            z = jnp.where(dist >= 0, jnp.where(dist <= win, z + on_grid, MASKED), MASKED)
            m_old = m_ref[...]
            m_new = jnp.maximum(m_old, jnp.max(z, axis=-1, keepdims=True))
            p = jnp.exp(z - m_new)
            a = jnp.exp(m_old - m_new)
            l_ref[...] = a * l_ref[...] + jnp.sum(p, axis=-1, keepdims=True)
            acc_ref[...] = a * acc_ref[...] + _dot(p.astype(BF16), v_ref[pl.ds(k0, t), :])
            m_ref[...] = m_new
            return carry

        lax.fori_loop(0, n_tiles, tile, 0)

    out = acc_ref[...] / l_ref[...]
    o_ref[...] = _merge_heads(out, t).astype(o_ref.dtype)


def _dilated_attention(dq, dk, dv):
    s = dq.shape[0]
    t = DIL_TILE
    assert s % t == 0 and all(w % t == 0 and t % d == 0 for w, d in zip(WINDOWS, DILATIONS))
    rows = pl.BlockSpec((t, GROUP_W), lambda i: (i, 0))
    return pl.pallas_call(
        _dil_kernel,
        out_shape=jax.ShapeDtypeStruct((s, GROUP_W), BF16),
        grid=(s // t,),
        in_specs=[rows, _resident((s, GROUP_W)), _resident((s, GROUP_W))],
        out_specs=rows,
        scratch_shapes=[pltpu.VMEM((N_HEADS * t, 1), F32), pltpu.VMEM((N_HEADS * t, 1), F32),
                        pltpu.VMEM((N_HEADS * t, GROUP_W), F32)],
        compiler_params=_params("parallel"),
        name="dilated_attn",
    )(dq, dk, dv)


def _sb_kernel(q_ref, k_ref, v_ref, o_ref, acc_ref, carry_ref):
    t = SB_TILE
    rows = N_HEADS * t
    qi = pl.program_id(0)
    qs = _stack_heads(q_ref[...], t)
    acc_ref[...] = jnp.zeros(acc_ref.shape, F32)
    carry_ref[...] = jnp.zeros(carry_ref.shape, F32)
    r = lax.broadcasted_iota(jnp.int32, (rows, t), 0)
    col = lax.broadcasted_iota(jnp.int32, (rows, t), 1)
    rc = _mod_pow2(r, t) - col
    kj =lax.broadcasted_iota(jnp.int32, (t, t), 0)
    ks = lax.broadcasted_iota(jnp.int32, (t, t), 1)
    later = jnp.where(kj > ks, 1.0, 0.0).astype(BF16)

    def tile(jj, c):
        k0 = pl.multiple_of((qi - jj) * t, t)
        z = _dot_nt(qs, k_ref[pl.ds(k0, t), :])
        softplus = jnp.maximum(z, 0.0) + jnp.log(1.0 + jnp.exp(-jnp.abs(z)))
        causal = rc > -(jj * t)
        log_keep = jnp.where(causal, -softplus, 0.0)
        log_beta = z - softplus
        within = _dot(log_keep.astype(BF16), later)
        a = jnp.where(causal, jnp.exp(log_beta + within + carry_ref[...]), 0.0)
        acc_ref[...] += _dot(a.astype(BF16), v_ref[pl.ds(k0, t), :])
        carry_ref[...] += jnp.sum(log_keep, axis=-1, keepdims=True)
        return c

    lax.fori_loop(0, qi + 1, tile, 0)
    o_ref[...] = _merge_heads(acc_ref[...], t).astype(o_ref.dtype)


def _stick_breaking(sq, sk, sv):
    s = sq.shape[0]
    t = SB_TILE
    assert s % t == 0
    rows = pl.BlockSpec((t, GROUP_W), lambda i: (i, 0))
    return pl.pallas_call(
        _sb_kernel,
        out_shape=jax.ShapeDtypeStruct((s, GROUP_W), BF16),
        grid=(s // t,),
        in_specs=[rows, _resident((s, GROUP_W)), _resident((s, GROUP_W))],
        out_specs=rows,
        scratch_shapes=[pltpu.VMEM((N_HEADS * t, GROUP_W), F32), pltpu.VMEM((N_HEADS * t, 1), F32)],
        compiler_params=_params("parallel"),
        name="stickbreak_attn",
    )(sq, sk, sv)


def _outproj_kernel(x_ref, yr_ref, yd_ref, ys_ref, mod_ref, ln_ref, w_ref, o_ref, *, alpha):
    vw, g = RET_VW, GROUP_W
    y = (_dot(yr_ref[...], w_ref[0:vw, :]) + _dot(yd_ref[...], w_ref[vw:vw + g, :])
         + _dot(ys_ref[...], w_ref[vw + g:vw + 2 * g, :]))
    o_ref[...] = _post_norm(x_ref[...], y, mod_ref[2:3, :], 1.0, ln_ref[0:1, :], ln_ref[1:2, :], alpha)


def _out_projection(x, y_ret, y_dil, y_sb, mod3, ln2, w_out, alpha):
    s, d = x.shape

    def rows(width):
        return pl.BlockSpec((ROW_TILE, width), lambda i: (i, 0))

    return pl.pallas_call(
        functools.partial(_outproj_kernel, alpha=alpha),
        out_shape=jax.ShapeDtypeStruct((s, d), F32),
        grid=(s // ROW_TILE,),
        in_specs=[rows(d), rows(RET_VW), rows(GROUP_W), rows(GROUP_W),
                  _resident((3, d)), _resident((2, d)), _resident(w_out.shape)],
        out_specs=rows(d),
        compiler_params=_params("parallel"),
        name="mixer_outproj",
    )(x, y_ret, y_dil, y_sb, mod3, ln2, w_out)


def kernel(x, c, w_ada, b_ada, ln_gain, ln_bias, ffn1_w_gate, ffn1_w_up, ffn1_w_down,
           w_in, w_out, ffn2_w_gate, ffn2_w_up, ffn2_w_down):
    batch, s, d = x.shape
    depth = w_ada.shape[0]
    assert batch == 1 and c.shape == (1, d)
    alpha = (2.0 * depth) ** 0.25
    half = HEAD_DIM // 2
    ret_freq = 1.0 / (ROPE_THETA ** jnp.linspace(0.0, 1.0, half, dtype=F32))
    rope_freq = 1.0 / (ROPE_THETA ** (jnp.arange(0, HEAD_DIM, 2, dtype=F32) / HEAD_DIM))
    tables = _rope_table(ret_freq, s) + _rope_table(rope_freq, s)
    mod = _modulation(c, w_ada, b_ada)
    ln = jnp.stack([ln_gain, ln_bias], axis=2)
    xs = x[0]
    for l in range(depth):
        xs = _ffn_sublayer(xs, mod[l, 0:3], ln[l, 0], ffn1_w_gate[l].astype(BF16),
                           ffn1_w_up[l].astype(BF16), ffn1_w_down[l].astype(BF16), alpha)
        rq, rk, rv, rg, dq, dk, dv, sq, sk, sv = _in_projection(xs, mod[l, 3:6], w_in[l].astype(BF16), tables)
        y_ret = _retention(rq, rk, rv, rg)
        y_dil = _dilated_attention(dq, dk, dv)
        y_sb = _stick_breaking(sq, sk, sv)
        xs = _out_projection(xs, y_ret, y_dil, y_sb, mod[l, 3:6], ln[l, 1], w_out[l].astype(BF16), alpha)
        xs = _ffn_sublayer(xs, mod[l, 6:9], ln[l, 2], ffn2_w_gate[l].astype(BF16),
                           ffn2_w_up[l].astype(BF16), ffn2_w_down[l].astype(BF16), alpha)
    return xs[None]
```

```python
import functools
import math

import jax
import jax.numpy as jnp
from jax import lax
from jax.experimental import pallas as pl
from jax.experimental.pallas import tpu as pltpu

HEAD_DIM = 64
N_HEADS = 4
RET_DV = 128
GROUP_W = N_HEADS * HEAD_DIM
RET_VW = N_HEADS * RET_DV
WINDOWS = (128, 512, 2048)
DILATIONS = (1, 4, 16)
ROPE_THETA = 10000.0
LN_EPS = 1e-5
GN_EPS = 1e-6
FFN_RES = 0.5
N_MOD = 9
LOG_GAMMA = tuple(math.log1p(-(2.0 ** (-5 - h))) for h in range(N_HEADS))

V7X_LANES = 128
V7X_VMEM_LIMIT_BYTES = 56 * 1024 * 1024
ROW_TILE = 512
FFN_CHUNKS = 2
RET_BLOCK = 256
DIL_TILE = 128
SB_TILE = 256
MASKED = -1e30
SB_STOP = -110.0

F32 = jnp.float32
BF16 = jnp.bfloat16


def _params(*sem):
    return pltpu.CompilerParams(dimension_semantics=sem, vmem_limit_bytes=V7X_VMEM_LIMIT_BYTES)


def _resident(shape):
    return pl.BlockSpec(shape, lambda *_: (0,) * len(shape), pipeline_mode=pl.Buffered(1))


def _dot(a, b):
    return jnp.dot(a, b, preferred_element_type=F32)


def _dot_nt(a, b):
    return lax.dot_general(a, b, (((1,), (1,)), ((), ())), preferred_element_type=F32)


def _silu(v):
    return v * jax.nn.sigmoid(v)


def _post_norm(x, y, gate, res_w, gain, bias, alpha):
    z = alpha * x + (res_w * (1.0 + gate)) * y
    mu = jnp.mean(z, axis=-1, keepdims=True)
    zc = z - mu
    var = jnp.mean(zc * zc, axis=-1, keepdims=True)
    return zc * lax.rsqrt(var + LN_EPS) * gain + bias


def _mod_kernel(c_ref, w_ref, b_ref, o_ref):
    c = c_ref[...]
    o_ref[...] = jnp.sum(_silu(c) * w_ref[...], axis=0, keepdims=True) + b_ref[...]


def _modulation(c, w_ada, b_ada):
    depth, d, n = w_ada.shape
    tn = d
    out = pl.pallas_call(
        _mod_kernel,
        out_shape=jax.ShapeDtypeStruct((depth, 1, n), F32),
        grid=(depth, n // tn),
        in_specs=[
            pl.BlockSpec((d, 1), lambda l, j: (0, 0)),
            pl.BlockSpec((None, d, tn), lambda l, j: (l, 0, j)),
            pl.BlockSpec((None, 1, tn), lambda l, j: (l, 0, j)),
        ],
        out_specs=pl.BlockSpec((None, 1, tn), lambda l, j: (l, 0, j)),
        compiler_params=_params("parallel", "parallel"),
        name="adaln_mod",
    )(c.reshape(d, 1), w_ada, b_ada.reshape(depth, 1, n))
    return out.reshape(depth, N_MOD, d)


def _ffn_kernel(x_ref, mod_ref, ln_ref, wg_ref, wu_ref, wd_ref, o_ref, *, alpha):
    x = x_ref[...]
    shift, scale, gate = mod_ref[0:1, :], mod_ref[1:2, :], mod_ref[2:3, :]
    h = (x * (1.0 + scale) + shift).astype(BF16)
    d_ff = wg_ref.shape[1]
    fc = d_ff // FFN_CHUNKS
    y = jnp.zeros(x.shape, F32)
    for ci in range(FFN_CHUNKS):
        cols = slice(ci * fc, (ci + 1) * fc)
        act = _silu(_dot(h, wg_ref[:, cols])) * _dot(h, wu_ref[:, cols])
        y = y + _dot(act.astype(BF16), wd_ref[cols, :])
    o_ref[...] = _post_norm(x, y, gate, FFN_RES, ln_ref[0:1, :], ln_ref[1:2, :], alpha)


def _ffn_sublayer(x, mod3, ln2, wg, wu, wd, alpha):
    s, d = x.shape
    d_ff = wg.shape[1]
    assert d_ff % (FFN_CHUNKS * V7X_LANES) == 0 and s % ROW_TILE == 0
    row = pl.BlockSpec((ROW_TILE, d), lambda i: (i, 0))
    return pl.pallas_call(
        functools.partial(_ffn_kernel, alpha=alpha),
        out_shape=jax.ShapeDtypeStruct((s, d), F32),
        grid=(s // ROW_TILE,),
        in_specs=[row, _resident((3, d)), _resident((2, d)),
                  _resident((d, d_ff)), _resident((d, d_ff)), _resident((d_ff, d))],
        out_specs=row,
        compiler_params=_params("parallel"),
        name="ffn_postnorm",
    )(x, mod3, ln2, wg, wu, wd)


def _rope(v, cos, sin):
    lane = lax.broadcasted_iota(jnp.int32, cos.shape, 1)
    first_half = (lane & (HEAD_DIM - 1)) < (HEAD_DIM // 2)
    halves = []
    for hf in range(GROUP_W // V7X_LANES):
        vh = v[:, hf * V7X_LANES:(hf + 1) * V7X_LANES]
        partner = jnp.where(first_half,
                            pltpu.roll(vh, V7X_LANES - HEAD_DIM // 2, 1),
                            pltpu.roll(vh, HEAD_DIM // 2, 1))
        halves.append(vh * cos + partner * sin)
    return jnp.concatenate(halves, axis=1)


def _inproj_kernel(x_ref, mod_ref, w_ref, cr_ref, sr_ref, cd_ref, sd_ref,
                   rq_ref, rk_ref, rv_ref, rg_ref, dq_ref, dk_ref, dv_ref, sq_ref, sk_ref, sv_ref):
    shift, scale = mod_ref[0:1, :], mod_ref[1:2, :]
    h = (x_ref[...] * (1.0 + scale) + shift).astype(BF16)
    qk_scale = HEAD_DIM ** -0.5
    g, vw = GROUP_W, RET_VW

    def proj(start, width):
        return _dot(h, w_ref[:, start:start + width])

    cr, sr, cd, sd = cr_ref[...], sr_ref[...], cd_ref[...], sd_ref[...]
    rq_ref[...] = _rope(proj(0, g), cr, sr).astype(BF16)
    rk_ref[...] = (_rope(proj(g, g), cr, sr) * qk_scale).astype(BF16)
    rv_ref[...] = proj(2 * g, vw).astype(BF16)
    rg_ref[...] = proj(2 * g + vw, vw)
    base = 2 * g + 2 * vw
    dq_ref[...] = (_rope(proj(base, g), cd, sd) * qk_scale).astype(BF16)
    dk_ref[...] = _rope(proj(base + g, g), cd, sd).astype(BF16)
    dv_ref[...] = proj(base + 2 * g, g).astype(BF16)
    sq_ref[...] = (proj(base + 3 * g, g) * qk_scale).astype(BF16)
    sk_ref[...] = proj(base + 4 * g, g).astype(BF16)
    sv_ref[...] = proj(base + 5 * g, g).astype(BF16)


def _in_projection(x, mod3, w_in, tables):
    s, d = x.shape
    n = w_in.shape[1]
    g, vw = GROUP_W, RET_VW
    assert n == 8 * g + 2 * vw

    def rows(width):
        return pl.BlockSpec((ROW_TILE, width), lambda i: (i, 0))

    widths = (g, g, vw, vw, g, g, g, g, g, g)
    dtypes = (BF16, BF16, BF16, F32, BF16, BF16, BF16, BF16, BF16, BF16)
    return pl.pallas_call(
        _inproj_kernel,
        out_shape=[jax.ShapeDtypeStruct((s, w), t) for w, t in zip(widths, dtypes)],
        grid=(s // ROW_TILE,),
        in_specs=[rows(d), _resident((3, d)), _resident((d, n))] + [rows(V7X_LANES)] * 4,
        out_specs=[rows(w) for w in widths],
        compiler_params=_params("parallel"),
        name="mixer_inproj",
    )(x, mod3, w_in, *tables)


def _rope_table(inv_freq, s):
    ang = jnp.arange(s, dtype=F32)[:, None] * inv_freq[None, :]
    cos, sin = jnp.cos(ang), jnp.sin(ang)
    reps = V7X_LANES // HEAD_DIM
    return (jnp.tile(jnp.concatenate([cos, cos], 1), (1, reps)),
            jnp.tile(jnp.concatenate([-sin, sin], 1), (1, reps)))


def _head_of(idx, width):
    assert width & (width - 1) == 0
    return lax.shift_right_logical(idx, width.bit_length() - 1)


def _mod_pow2(idx, width):
    assert width & (width - 1) == 0
    return idx & (width - 1)


def _log_gamma_of(head):
    lg = jnp.full(head.shape, LOG_GAMMA[N_HEADS - 1], F32)
    for hh in range(N_HEADS - 2, -1, -1):
        lg = jnp.where(head == hh, LOG_GAMMA[hh], lg)
    return lg


def _stack_heads(q, rows):
    r = lax.broadcasted_iota(jnp.int32, (N_HEADS * rows, GROUP_W), 0)
    l = lax.broadcasted_iota(jnp.int32, (N_HEADS * rows, GROUP_W), 1)
    keep = _head_of(r, rows) == _head_of(l, HEAD_DIM)
    qf = q.astype(F32)
    return jnp.where(keep, jnp.concatenate([qf] * N_HEADS, axis=0), 0.0).astype(q.dtype)


def _merge_heads(stacked, rows):
    l = lax.broadcasted_iota(jnp.int32, (rows, GROUP_W), 1)
    head = _head_of(l, HEAD_DIM)
    out = jnp.zeros((rows, GROUP_W), stacked.dtype)
    for hh in range(N_HEADS):
        out = jnp.where(head == hh, stacked[hh * rows:(hh + 1) * rows, :], out)
    return out


def _ret_kernel(q_ref, k_ref, v_ref, g_ref, o_ref, state_ref, decay_ref, xi_ref, zeta_ref, gc_ref):
    c = RET_BLOCK
    rows = N_HEADS * c

    @pl.when(pl.program_id(0) == 0)
    def _init():
        state_ref[...] = jnp.zeros(state_ref.shape, F32)
        r = lax.broadcasted_iota(jnp.int32, (rows, c), 0)
        j = lax.broadcasted_iota(jnp.int32, (rows, c), 1)
        diff = (_mod_pow2(r, c) - j).astype(F32)
        lg = _log_gamma_of(_head_of(r, c))
        decay_ref[...] = jnp.where(diff >= 0, jnp.exp(lg * jnp.maximum(diff, 0.0)), 0.0)
        r = lax.broadcasted_iota(jnp.int32, (rows, RET_DV), 0)
        xi_ref[...] = jnp.exp(_log_gamma_of(_head_of(r, c)) * (_mod_pow2(r, c).astype(F32) + 1.0))
        i = lax.broadcasted_iota(jnp.int32, (c, GROUP_W), 0)
        l = lax.broadcasted_iota(jnp.int32, (c, GROUP_W), 1)
        zeta_ref[...] = jnp.exp(_log_gamma_of(_head_of(l, HEAD_DIM)) * (c - 1.0 - i.astype(F32)))
        r = lax.broadcasted_iota(jnp.int32, (GROUP_W, RET_DV), 0)
        gc_ref[...] = jnp.exp(_log_gamma_of(_head_of(r, HEAD_DIM)) * float(c))

    q, k, v = q_ref[...], k_ref[...], v_ref[...]
    qs = _stack_heads(q, c)
    scores = (_dot_nt(qs, k) * decay_ref[...]).astype(BF16)
    inner = jnp.concatenate(
        [_dot(scores[hh * c:(hh + 1) * c, :], v[:, hh * RET_DV:(hh + 1) * RET_DV]) for hh in range(N_HEADS)],
        axis=0)
    state = state_ref[...]
    o = inner + _dot(qs, state.astype(BF16)) * xi_ref[...]
    mu = jnp.mean(o, axis=-1, keepdims=True)
    oc = o - mu
    var = jnp.mean(oc * oc, axis=-1, keepdims=True)
    on = oc * lax.rsqrt(var + GN_EPS)
    for hh in range(N_HEADS):
        lanes = slice(hh * RET_DV, (hh + 1) * RET_DV)
        o_ref[:, lanes] = (_silu(g_ref[:, lanes]) * on[hh * c:(hh + 1) * c, :]).astype(o_ref.dtype)
    kz_t = (k.astype(F32) * zeta_ref[...]).T.astype(BF16)
    kv = _dot(kz_t, v)
    gc = gc_ref[...]
    for hh in range(N_HEADS):
        rws = slice(hh * HEAD_DIM, (hh + 1) * HEAD_DIM)
        state_ref[rws, :] = state[rws, :] * gc[rws, :] + kv[rws, hh * RET_DV:(hh + 1) * RET_DV]


def _retention(rq, rk, rv, rg):
    s = rq.shape[0]
    c = RET_BLOCK
    assert s % c == 0

    def rows(width):
        return pl.BlockSpec((c, width), lambda i: (i, 0))

    return pl.pallas_call(
        _ret_kernel,
        out_shape=jax.ShapeDtypeStruct((s, RET_VW), BF16),
        grid=(s // c,),
        in_specs=[rows(GROUP_W), rows(GROUP_W), rows(RET_VW), rows(RET_VW)],
        out_specs=rows(RET_VW),
        scratch_shapes=[pltpu.VMEM((GROUP_W, RET_DV), F32), pltpu.VMEM((N_HEADS * c, c), F32),
                        pltpu.VMEM((N_HEADS * c, RET_DV), F32), pltpu.VMEM((c, GROUP_W), F32),
                        pltpu.VMEM((GROUP_W, RET_DV), F32)],
        compiler_params=_params("arbitrary"),
        name="retention",
    )(rq, rk, rv, rg)


def _dil_kernel(q_ref, k_ref, v_ref, o_ref, m_ref, l_ref, acc_ref):
    t = DIL_TILE
    rows = N_HEADS * t
    qi = pl.program_id(0)
    q0 = qi * t
    qs = _stack_heads(q_ref[...], t)
    m_ref[...] = jnp.full(m_ref.shape, MASKED, F32)
    l_ref[...] = jnp.zeros(l_ref.shape, F32)
    acc_ref[...] = jnp.zeros(acc_ref.shape, F32)
    r = lax.broadcasted_iota(jnp.int32, (rows, t), 0)
    col = lax.broadcasted_iota(jnp.int32, (rows, t), 1)
    rc = _mod_pow2(r, t) - col

    for dil, win in zip(DILATIONS, WINDOWS):
        on_grid = jnp.where((rc & (dil - 1)) == 0, 0.0, MASKED)
        n_tiles = jnp.minimum(win // t, qi) + 1

        def tile(jj, carry, on_grid=on_grid, win=win):
            back = jj * t
            k0 = pl.multiple_of(q0 - back, t)
            z = _dot_nt(qs, k_ref[pl.ds(k0, t), :])
            dist = rc + back
            z = jnp.where(dist >= 0, jnp.where(dist <= win, z + on_grid, MASKED), MASKED)
            m_old = m_ref[...]
            m_new = jnp.maximum(m_old, jnp.max(z, axis=-1, keepdims=True))
            p = jnp.exp(z - m_new)
            a = jnp.exp(m_old - m_new)
            l_ref[...] = a * l_ref[...] + jnp.sum(p, axis=-1, keepdims=True)
            acc_ref[...] = a * acc_ref[...] + _dot(p.astype(BF16), v_ref[pl.ds(k0, t), :])
            m_ref[...] = m_new
            return carry

        lax.fori_loop(0, n_tiles, tile, 0)

    out = acc_ref[...] / l_ref[...]
    o_ref[...] = _merge_heads(out, t).astype(o_ref.dtype)


def _dilated_attention(dq, dk, dv):
    s = dq.shape[0]
    t = DIL_TILE
    assert s % t == 0 and all(w % t == 0 and t % d == 0 for w, d in zip(WINDOWS, DILATIONS))
    rows = pl.BlockSpec((t, GROUP_W), lambda i: (i, 0))
    return pl.pallas_call(
        _dil_kernel,
        out_shape=jax.ShapeDtypeStruct((s, GROUP_W), BF16),
        grid=(s // t,),
        in_specs=[rows, _resident((s, GROUP_W)), _resident((s, GROUP_W))],
        out_specs=rows,
        scratch_shapes=[pltpu.VMEM((N_HEADS * t, 1), F32), pltpu.VMEM((N_HEADS * t, 1), F32),
                        pltpu.VMEM((N_HEADS * t, GROUP_W), F32)],
        compiler_params=_params("parallel"),
        name="dilated_attn",
    )(dq, dk, dv)


def _sb_kernel(q_ref, k_ref, v_ref, o_ref, acc_ref, carry_ref):
    t = SB_TILE
    rows = N_HEADS * t
    qi = pl.program_id(0)
    qs = _stack_heads(q_ref[...], t)
    acc_ref[...] = jnp.zeros(acc_ref.shape, F32)
    carry_ref[...] = jnp.zeros(carry_ref.shape, F32)
    r = lax.broadcasted_iota(jnp.int32, (rows, t), 0)
    col = lax.broadcasted_iota(jnp.int32, (rows, t), 1)
    rc = _mod_pow2(r, t) - col
    kj =lax.broadcasted_iota(jnp.int32, (t, t), 0)
    ks = lax.broadcasted_iota(jnp.int32, (t, t), 1)
    later = jnp.where(kj > ks, 1.0, 0.0).astype(BF16)

    def unfinished(state):
        jj, carry_max = state
        return jnp.logical_and(jj <= qi, carry_max > SB_STOP)

    def tile(state):
        jj, _ = state
        k0 = pl.multiple_of((qi - jj) * t, t)
        z = _dot_nt(qs, k_ref[pl.ds(k0, t), :])
        softplus = jnp.maximum(z, 0.0) + jnp.log(1.0 + jnp.exp(-jnp.abs(z)))
        causal = rc > -(jj * t)
        log_keep = jnp.where(causal, -softplus, 0.0)
        log_beta = z - softplus
        within = _dot(log_keep.astype(BF16), later)
        a = jnp.where(causal, jnp.exp(log_beta + within + carry_ref[...]), 0.0)
        acc_ref[...] += _dot(a.astype(BF16), v_ref[pl.ds(k0, t), :])
        carry = carry_ref[...] + jnp.sum(log_keep, axis=-1, keepdims=True)
        carry_ref[...] = carry
        return jj + 1, jnp.max(carry)

    lax.while_loop(unfinished, tile, (jnp.int32(0), jnp.float32(0.0)))
    o_ref[...] = _merge_heads(acc_ref[...], t).astype(o_ref.dtype)


def _stick_breaking(sq, sk, sv):
    s = sq.shape[0]
    t = SB_TILE
    assert s % t == 0
    rows = pl.BlockSpec((t, GROUP_W), lambda i: (i, 0))
    return pl.pallas_call(
        _sb_kernel,
        out_shape=jax.ShapeDtypeStruct((s, GROUP_W), BF16),
        grid=(s // t,),
        in_specs=[rows, _resident((s, GROUP_W)), _resident((s, GROUP_W))],
        out_specs=rows,
        scratch_shapes=[pltpu.VMEM((N_HEADS * t, GROUP_W), F32), pltpu.VMEM((N_HEADS * t, 1), F32)],
        compiler_params=_params("parallel"),
        name="stickbreak_attn",
    )(sq, sk, sv)


def _outproj_kernel(x_ref, yr_ref, yd_ref, ys_ref, mod_ref, ln_ref, w_ref, o_ref, *, alpha):
    vw, g = RET_VW, GROUP_W
    y = (_dot(yr_ref[...], w_ref[0:vw, :]) + _dot(yd_ref[...], w_ref[vw:vw + g, :])
         + _dot(ys_ref[...], w_ref[vw + g:vw + 2 * g, :]))
    o_ref[...] = _post_norm(x_ref[...], y, mod_ref[2:3, :], 1.0, ln_ref[0:1, :], ln_ref[1:2, :], alpha)


def _out_projection(x, y_ret, y_dil, y_sb, mod3, ln2, w_out, alpha):
    s, d = x.shape

    def rows(width):
        return pl.BlockSpec((ROW_TILE, width), lambda i: (i, 0))

    return pl.pallas_call(
        functools.partial(_outproj_kernel, alpha=alpha),
        out_shape=jax.ShapeDtypeStruct((s, d), F32),
        grid=(s // ROW_TILE,),
        in_specs=[rows(d), rows(RET_VW), rows(GROUP_W), rows(GROUP_W),
                  _resident((3, d)), _resident((2, d)), _resident(w_out.shape)],
        out_specs=rows(d),
        compiler_params=_params("parallel"),
        name="mixer_outproj",
    )(x, y_ret, y_dil, y_sb, mod3, ln2, w_out)


def kernel(x, c, w_ada, b_ada, ln_gain, ln_bias, ffn1_w_gate, ffn1_w_up, ffn1_w_down,
           w_in, w_out, ffn2_w_gate, ffn2_w_up, ffn2_w_down):
    batch, s, d = x.shape
    depth = w_ada.shape[0]
    assert batch == 1 and c.shape == (1, d)
    alpha = (2.0 * depth) ** 0.25
    half = HEAD_DIM // 2
    ret_freq = 1.0 / (ROPE_THETA ** jnp.linspace(0.0, 1.0, half, dtype=F32))
    rope_freq = 1.0 / (ROPE_THETA ** (jnp.arange(0, HEAD_DIM, 2, dtype=F32) / HEAD_DIM))
    tables = _rope_table(ret_freq, s) + _rope_table(rope_freq, s)
    mod = _modulation(c, w_ada, b_ada)
    ln = jnp.stack([ln_gain, ln_bias], axis=2)
    xs = x[0]
    for l in range(depth):
        xs = _ffn_sublayer(xs, mod[l, 0:3], ln[l, 0], ffn1_w_gate[l].astype(BF16),
                           ffn1_w_up[l].astype(BF16), ffn1_w_down[l].astype(BF16), alpha)
        rq, rk, rv, rg, dq, dk, dv, sq, sk, sv = _in_projection(xs, mod[l, 3:6], w_in[l].astype(BF16), tables)
        y_ret = _retention(rq, rk, rv, rg)
        y_dil = _dilated_attention(dq, dk, dv)
        y_sb = _stick_breaking(sq, sk, sv)
        xs = _out_projection(xs, y_ret, y_dil, y_sb, mod[l, 3:6], ln[l, 1], w_out[l].astype(BF16), alpha)
        xs = _ffn_sublayer(xs, mod[l, 6:9], ln[l, 2], ffn2_w_gate[l].astype(BF16),
                           ffn2_w_up[l].astype(BF16), ffn2_w_down[l].astype(BF16), alpha)
    return xs[None]
```

```python
import functools
import math

import jax
import jax.numpy as jnp
from jax import lax
from jax.experimental import pallas as pl
from jax.experimental.pallas import tpu as pltpu

HEAD_DIM = 64
N_HEADS = 4
RET_DV = 128
GROUP_W = N_HEADS * HEAD_DIM
RET_VW = N_HEADS * RET_DV
WINDOWS = (128, 512, 2048)
DILATIONS = (1, 4, 16)
ROPE_THETA = 10000.0
LN_EPS = 1e-5
GN_EPS = 1e-6
FFN_RES = 0.5
N_MOD = 9
LOG_GAMMA = tuple(math.log1p(-(2.0 ** (-5 - h))) for h in range(N_HEADS))

V7X_LANES = 128
N_SLABS = GROUP_W // V7X_LANES
V7X_VMEM_LIMIT_BYTES = 56 * 1024 * 1024
ROW_TILE = 512
FFN_ROW_TILE = 1024
FFN_CHUNKS = 11
RET_BLOCK = 256
DIL_TILE = 128
DIL_BLOCK = DIL_TILE * max(DILATIONS)
SB_TILE = 256
MASKED = -1e30
SB_STOP = -110.0

F32 = jnp.float32
BF16 = jnp.bfloat16


def _params(*sem):
    return pltpu.CompilerParams(dimension_semantics=sem, vmem_limit_bytes=V7X_VMEM_LIMIT_BYTES)


def _resident(shape):
    return pl.BlockSpec(shape, lambda *_: (0,) * len(shape), pipeline_mode=pl.Buffered(1))


def _dot(a, b):
    return jnp.dot(a, b, preferred_element_type=F32)


def _dot_nt(a, b):
    return lax.dot_general(a, b, (((1,), (1,)), ((), ())), preferred_element_type=F32)


def _silu(v):
    return v * jax.nn.sigmoid(v)


def _post_norm(x, y, gate, res_w, gain, bias, alpha):
    z = alpha * x + (res_w * (1.0 + gate)) * y
    mu = jnp.mean(z, axis=-1, keepdims=True)
    zc = z - mu
    var = jnp.mean(zc * zc, axis=-1, keepdims=True)
    return zc * lax.rsqrt(var + LN_EPS) * gain + bias


def _mod_kernel(c_ref, w_ref, b_ref, o_ref):
    c = c_ref[...]
    o_ref[...] = jnp.sum(_silu(c) * w_ref[...], axis=0, keepdims=True) + b_ref[...]


def _modulation(c, w_ada, b_ada):
    depth, d, n = w_ada.shape
    tn = d
    out = pl.pallas_call(
        _mod_kernel,
        out_shape=jax.ShapeDtypeStruct((depth, 1, n), F32),
        grid=(depth, n // tn),
        in_specs=[
            pl.BlockSpec((d, 1), lambda l, j: (0, 0)),
            pl.BlockSpec((None, d, tn), lambda l, j: (l, 0, j)),
            pl.BlockSpec((None, 1, tn), lambda l, j: (l, 0, j)),
        ],
        out_specs=pl.BlockSpec((None, 1, tn), lambda l, j: (l, 0, j)),
        compiler_params=_params("parallel", "parallel"),
        name="adaln_mod",
    )(c.reshape(d, 1), w_ada, b_ada.reshape(depth, 1, n))
    return out.reshape(depth, N_MOD, d)


def _ffn_kernel(x_ref, mod_ref, ln_ref, wg_ref, wu_ref, wd_ref, o_ref, *, alpha):
    shift, scale, gate = mod_ref[0:1, :], mod_ref[1:2, :], mod_ref[2:3, :]
    d_ff = wg_ref.shape[1]
    fc = d_ff // FFN_CHUNKS
    x = x_ref[...]
    h = (x * (1.0 + scale) + shift).astype(BF16)
    y = jnp.zeros(x.shape, F32)
    for ci in range(FFN_CHUNKS):
        cols = slice(ci * fc, (ci + 1) * fc)
        act = _silu(_dot(h, wg_ref[:, cols])) * _dot(h, wu_ref[:, cols])
        y = y + _dot(act.astype(BF16), wd_ref[cols, :])
    o_ref[...] = _post_norm(x, y, gate, FFN_RES, ln_ref[0:1, :], ln_ref[1:2, :], alpha)


def _ffn_sublayer(x, mod3, ln2, wg, wu, wd, alpha):
    s, d = x.shape
    d_ff = wg.shape[1]
    assert d_ff % (FFN_CHUNKS * V7X_LANES) == 0 and s % FFN_ROW_TILE == 0
    row = pl.BlockSpec((FFN_ROW_TILE, d), lambda i: (i, 0))
    return pl.pallas_call(
        functools.partial(_ffn_kernel, alpha=alpha),
        out_shape=jax.ShapeDtypeStruct((s, d), F32),
        grid=(s // FFN_ROW_TILE,),
        in_specs=[row, _resident((3, d)), _resident((2, d)),
                  _resident((d, d_ff)), _resident((d, d_ff)), _resident((d_ff, d))],
        out_specs=row,
        compiler_params=_params("parallel"),
        name="ffn_postnorm",
    )(x, mod3, ln2, wg, wu, wd)


def _rope(v, cos, sin):
    lane = lax.broadcasted_iota(jnp.int32, cos.shape, 1)
    first_half = (lane & (HEAD_DIM - 1)) < (HEAD_DIM // 2)
    halves = []
    for hf in range(GROUP_W // V7X_LANES):
        vh = v[:, hf * V7X_LANES:(hf + 1) * V7X_LANES]
        partner = jnp.where(first_half,
                            pltpu.roll(vh, V7X_LANES - HEAD_DIM // 2, 1),
                            pltpu.roll(vh, HEAD_DIM // 2, 1))
        halves.append(vh * cos + partner * sin)
    return jnp.concatenate(halves, axis=1)


def _inproj_kernel(x_ref, mod_ref, w_ref, cr_ref, sr_ref, cd_ref, sd_ref,
                   rq_ref, rk_ref, rv_ref, rg_ref, dq_ref, dk_ref, dv_ref, sq_ref, sk_ref, sv_ref):
    shift, scale = mod_ref[0:1, :], mod_ref[1:2, :]
    h = (x_ref[...] * (1.0 + scale) + shift).astype(BF16)
    qk_scale = HEAD_DIM ** -0.5
    g, vw = GROUP_W, RET_VW

    full = _dot(h, w_ref[...])

    def proj(start, width):
        return full[:, start:start + width]

    cr, sr, cd, sd = cr_ref[...], sr_ref[...], cd_ref[...], sd_ref[...]
    rq_ref[...] = _rope(proj(0, g), cr, sr).astype(BF16)
    rk_ref[...] = (_rope(proj(g, g), cr, sr) * qk_scale).astype(BF16)
    rv_ref[...] = proj(2 * g, vw).astype(BF16)
    rg_ref[...] = proj(2 * g + vw, vw)
    base = 2 * g + 2 * vw
    for ref, val in ((dq_ref, _rope(proj(base, g), cd, sd) * qk_scale),
                     (dk_ref, _rope(proj(base + g, g), cd, sd)), (dv_ref, proj(base + 2 * g, g))):
        for sl in range(N_SLABS):
            ref[sl] = val[:, sl * V7X_LANES:(sl + 1) * V7X_LANES]
    sq_ref[...] = (proj(base + 3 * g, g) * qk_scale).astype(BF16)
    sk_ref[...] = proj(base + 4 * g, g).astype(BF16)
    sv_ref[...] = proj(base + 5 * g, g).astype(BF16)


def _in_projection(x, mod3, w_in, tables):
    s, d = x.shape
    n = w_in.shape[1]
    g, vw = GROUP_W, RET_VW
    assert n == 8 * g + 2 * vw

    def rows(width):
        return pl.BlockSpec((ROW_TILE, width), lambda i: (i, 0))

    def flat(width, dtype):
        return jax.ShapeDtypeStruct((s, width), dtype), rows(width)

    slabs = (jax.ShapeDtypeStruct((N_SLABS, s, V7X_LANES), F32),
             pl.BlockSpec((N_SLABS, ROW_TILE, V7X_LANES), lambda i: (0, i, 0)))
    outs = (flat(g, BF16), flat(g, BF16), flat(vw, BF16), flat(vw, F32), slabs, slabs, slabs,
            flat(g, BF16), flat(g, BF16), flat(g, BF16))
    return pl.pallas_call(
        _inproj_kernel,
        out_shape=[o[0] for o in outs],
        grid=(s // ROW_TILE,),
        in_specs=[rows(d), _resident((3, d)), _resident((d, n))] + [rows(V7X_LANES)] * 4,
        out_specs=[o[1] for o in outs],
        compiler_params=_params("parallel"),
        name="mixer_inproj",
    )(x, mod3, w_in, *tables)


def _rope_table(inv_freq, s):
    ang = jnp.arange(s, dtype=F32)[:, None] * inv_freq[None, :]
    cos, sin = jnp.cos(ang), jnp.sin(ang)
    reps = V7X_LANES // HEAD_DIM
    return (jnp.tile(jnp.concatenate([cos, cos], 1), (1, reps)),
            jnp.tile(jnp.concatenate([-sin, sin], 1), (1, reps)))


def _head_of(idx, width):
    assert width & (width - 1) == 0
    return lax.shift_right_logical(idx, width.bit_length() - 1)


def _mod_pow2(idx, width):
    assert width & (width - 1) == 0
    return idx & (width - 1)


def _log_gamma_of(head):
    lg = jnp.full(head.shape, LOG_GAMMA[N_HEADS - 1], F32)
    for hh in range(N_HEADS - 2, -1, -1):
        lg = jnp.where(head == hh, LOG_GAMMA[hh], lg)
    return lg


def _stack_heads(q, rows):
    r = lax.broadcasted_iota(jnp.int32, (N_HEADS * rows, GROUP_W), 0)
    l = lax.broadcasted_iota(jnp.int32, (N_HEADS * rows, GROUP_W), 1)
    keep = _head_of(r, rows) == _head_of(l, HEAD_DIM)
    qf = q.astype(F32)
    return jnp.where(keep, jnp.concatenate([qf] * N_HEADS, axis=0), 0.0).astype(q.dtype)


def _merge_heads(stacked, rows):
    l = lax.broadcasted_iota(jnp.int32, (rows, GROUP_W), 1)
    head = _head_of(l, HEAD_DIM)
    out = jnp.zeros((rows, GROUP_W), stacked.dtype)
    for hh in range(N_HEADS):
        out = jnp.where(head == hh, stacked[hh * rows:(hh + 1) * rows, :], out)
    return out


def _ret_kernel(q_ref, k_ref, v_ref, g_ref, o_ref, state_ref, decay_ref, xi_ref, zeta_ref, gc_ref):
    c = RET_BLOCK
    rows = N_HEADS * c

    @pl.when(pl.program_id(0) == 0)
    def _init():
        state_ref[...] = jnp.zeros(state_ref.shape, F32)
        r = lax.broadcasted_iota(jnp.int32, (rows, c), 0)
        j = lax.broadcasted_iota(jnp.int32, (rows, c), 1)
        diff = (_mod_pow2(r, c) - j).astype(F32)
        lg = _log_gamma_of(_head_of(r, c))
        decay_ref[...] = jnp.where(diff >= 0, jnp.exp(lg * jnp.maximum(diff, 0.0)), 0.0)
        r = lax.broadcasted_iota(jnp.int32, (rows, RET_DV), 0)
        xi_ref[...] = jnp.exp(_log_gamma_of(_head_of(r, c)) * (_mod_pow2(r, c).astype(F32) + 1.0))
        i = lax.broadcasted_iota(jnp.int32, (c, GROUP_W), 0)
        l = lax.broadcasted_iota(jnp.int32, (c, GROUP_W), 1)
        zeta_ref[...] = jnp.exp(_log_gamma_of(_head_of(l, HEAD_DIM)) * (c - 1.0 - i.astype(F32)))
        r = lax.broadcasted_iota(jnp.int32, (GROUP_W, RET_DV), 0)
        gc_ref[...] = jnp.exp(_log_gamma_of(_head_of(r, HEAD_DIM)) * float(c))

    q, k, v = q_ref[...], k_ref[...], v_ref[...]
    qs = _stack_heads(q, c)
    scores = (_dot_nt(qs, k) * decay_ref[...]).astype(BF16)
    inner = jnp.concatenate(
        [_dot(scores[hh * c:(hh + 1) * c, :], v[:, hh * RET_DV:(hh + 1) * RET_DV]) for hh in range(N_HEADS)],
        axis=0)
    state = state_ref[...]
    o = inner + _dot(qs, state.astype(BF16)) * xi_ref[...]
    mu = jnp.mean(o, axis=-1, keepdims=True)
    oc = o - mu
    var = jnp.mean(oc * oc, axis=-1, keepdims=True)
    on = oc * lax.rsqrt(var + GN_EPS)
    for hh in range(N_HEADS):
        lanes = slice(hh * RET_DV, (hh + 1) * RET_DV)
        o_ref[:, lanes] = (_silu(g_ref[:, lanes]) * on[hh * c:(hh + 1) * c, :]).astype(o_ref.dtype)
    kz_t = (k.astype(F32) * zeta_ref[...]).T.astype(BF16)
    kv = _dot(kz_t, v)
    gc = gc_ref[...]
    for hh in range(N_HEADS):
        rws = slice(hh * HEAD_DIM, (hh + 1) * HEAD_DIM)
        state_ref[rws, :] = state[rws, :] * gc[rws, :] + kv[rws, hh * RET_DV:(hh + 1) * RET_DV]


def _retention(rq, rk, rv, rg):
    s = rq.shape[0]
    c = RET_BLOCK
    assert s % c == 0

    def rows(width):
        return pl.BlockSpec((c, width), lambda i: (i, 0))

    return pl.pallas_call(
        _ret_kernel,
        out_shape=jax.ShapeDtypeStruct((s, RET_VW), BF16),
        grid=(s // c,),
        in_specs=[rows(GROUP_W), rows(GROUP_W), rows(RET_VW), rows(RET_VW)],
        out_specs=rows(RET_VW),
        scratch_shapes=[pltpu.VMEM((GROUP_W, RET_DV), F32), pltpu.VMEM((N_HEADS * c, c), F32),
                        pltpu.VMEM((N_HEADS * c, RET_DV), F32), pltpu.VMEM((c, GROUP_W), F32),
                        pltpu.VMEM((GROUP_W, RET_DV), F32)],
        compiler_params=_params("arbitrary"),
        name="retention",
    )(rq, rk, rv, rg)


def _load_rows(ref, start, rows, stride):
    idx = pl.ds(start, rows) if stride == 1 else pl.ds(start, rows, stride=stride)
    return jnp.concatenate([ref[sl, idx, :] for sl in range(N_SLABS)], axis=1)


def _store_rows(ref, start, rows, stride, val):
    idx = pl.ds(start, rows) if stride == 1 else pl.ds(start, rows, stride=stride)
    for sl in range(N_SLABS):
        ref[sl, idx, :] = val[:, sl * V7X_LANES:(sl + 1) * V7X_LANES]


def _dil_kernel(q_ref, k_ref, v_ref, o_ref, kwin_ref, vwin_ref, part_o_ref, part_l_ref):
    t, blk = DIL_TILE, DIL_BLOCK
    b = pl.program_id(0)

    @pl.when(b == 0)
    def _first():
        kwin_ref[:, 0:blk, :] = jnp.zeros((N_SLABS, blk, V7X_LANES), F32)
        vwin_ref[:, 0:blk, :] = jnp.zeros((N_SLABS, blk, V7X_LANES), F32)

    @pl.when(b > 0)
    def _shift():
        kwin_ref[:, 0:blk, :] = kwin_ref[:, blk:2 * blk, :]
        vwin_ref[:, 0:blk, :] = vwin_ref[:, blk:2 * blk, :]

    kwin_ref[:, blk:2 * blk, :] = k_ref[...]
    vwin_ref[:, blk:2 * blk, :] = v_ref[...]

    rows = N_HEADS * t
    qi = _mod_pow2(lax.broadcasted_iota(jnp.int32, (rows, 2 * t), 0), t)
    kc = lax.broadcasted_iota(jnp.int32, (rows, 2 * t), 1)
    ahead = kc - qi
    in_band = jnp.logical_and(ahead >= 0, ahead <= t)

    for pi, dil in enumerate(DILATIONS):

        def unit(u, carry, pi=pi, dil=dil):
            res = _mod_pow2(u, dil)
            ct = lax.shift_right_logical(u, dil.bit_length() - 1)
            q_start = res + ct * (dil * t)
            k_start = blk + q_start - dil * t
            first_valid = jnp.where(jnp.logical_and(b == 0, ct == 0), t, 0)
            bias = jnp.where(jnp.logical_and(in_band, kc >= first_valid), 0.0, MASKED)
            qs = _stack_heads(_load_rows(q_ref, q_start, t, dil).astype(BF16), t)
            kk = _load_rows(kwin_ref, k_start, 2 * t, dil).astype(BF16)
            vv = _load_rows(vwin_ref, k_start, 2 * t, dil).astype(BF16)
            z = _dot_nt(qs, kk) + bias
            m = jnp.max(z, axis=-1, keepdims=True)
            p = jnp.exp(z - m)
            l = jnp.sum(p, axis=-1, keepdims=True)
            o = _dot(p.astype(BF16), vv) / l
            lse = jnp.broadcast_to(m + jnp.log(l), (rows, GROUP_W))
            _store_rows(part_o_ref.at[pi], q_start, t, dil, _merge_heads(o, t))
            _store_rows(part_l_ref.at[pi], q_start, t, dil, _merge_heads(lse, t))
            return carry

        lax.fori_loop(0, blk // t, unit, 0, unroll=2)

    def combine(ci, carry):
        r0 = pl.multiple_of(ci * t, t)
        for sl in range(N_SLABS):
            ls = [part_l_ref[pi, sl, pl.ds(r0, t), :] for pi in range(len(DILATIONS))]
            top = functools.reduce(jnp.maximum, ls)
            ws = [jnp.exp(x - top) for x in ls]
            num = sum(w * part_o_ref[pi, sl, pl.ds(r0, t), :] for pi, w in enumerate(ws))
            o_ref[pl.ds(r0, t), sl * V7X_LANES:(sl + 1) * V7X_LANES] = (num / sum(ws)).astype(o_ref.dtype)
        return carry

    lax.fori_loop(0, blk // t, combine, 0)


def _dilated_attention(dq, dk, dv):
    s = dq.shape[1]
    blk = DIL_BLOCK
    assert s % blk == 0 and all(w == DIL_TILE * d for w, d in zip(WINDOWS, DILATIONS))
    slab = pl.BlockSpec((N_SLABS, blk, V7X_LANES), lambda i: (0, i, 0))
    n_pat = len(DILATIONS)
    return pl.pallas_call(
        _dil_kernel,
        out_shape=jax.ShapeDtypeStruct((s, GROUP_W), BF16),
        grid=(s // blk,),
        in_specs=[slab, slab, slab],
        out_specs=pl.BlockSpec((blk, GROUP_W), lambda i: (i, 0)),
        scratch_shapes=[pltpu.VMEM((N_SLABS, 2 * blk, V7X_LANES), F32), pltpu.VMEM((N_SLABS, 2 * blk, V7X_LANES), F32),
                        pltpu.VMEM((n_pat, N_SLABS, blk, V7X_LANES), F32),
                        pltpu.VMEM((n_pat, N_SLABS, blk, V7X_LANES), F32)],
        compiler_params=_params("arbitrary"),
        name="dilated_attn",
    )(dq, dk, dv)


def _sb_kernel(q_ref, k_ref, v_ref, o_ref, acc_ref, carry_ref):
    t = SB_TILE
    rows = N_HEADS * t
    qi = pl.program_id(0)
    qs = _stack_heads(q_ref[...], t)
    acc_ref[...] = jnp.zeros(acc_ref.shape, F32)
    carry_ref[...] = jnp.zeros(carry_ref.shape, F32)
    r = lax.broadcasted_iota(jnp.int32, (rows, t), 0)
    col = lax.broadcasted_iota(jnp.int32, (rows, t), 1)
    rc = _mod_pow2(r, t) - col
    kj =lax.broadcasted_iota(jnp.int32, (t, t), 0)
    ks = lax.broadcasted_iota(jnp.int32, (t, t), 1)
    later = jnp.where(kj > ks, 1.0, 0.0).astype(BF16)

    def unfinished(state):
        jj, carry_max = state
        return jnp.logical_and(jj <= qi, carry_max > SB_STOP)

    def tile(state):
        jj, _ = state
        k0 = pl.multiple_of((qi - jj) * t, t)
        z = _dot_nt(qs, k_ref[pl.ds(k0, t), :])
        softplus = jnp.maximum(z, 0.0) + jnp.log(1.0 + jnp.exp(-jnp.abs(z)))
        causal = rc > -(jj * t)
        log_keep = jnp.where(causal, -softplus, 0.0)
        log_beta = z - softplus
        within = _dot(log_keep.astype(BF16), later)
        a = jnp.where(causal, jnp.exp(log_beta + within + carry_ref[...]), 0.0)
        acc_ref[...] += _dot(a.astype(BF16), v_ref[pl.ds(k0, t), :])
        carry = carry_ref[...] + jnp.sum(log_keep, axis=-1, keepdims=True)
        carry_ref[...] = carry
        return jj + 1, jnp.max(carry)

    lax.while_loop(unfinished, tile, (jnp.int32(0), jnp.float32(0.0)))
    o_ref[...] = _merge_heads(acc_ref[...], t).astype(o_ref.dtype)


def _stick_breaking(sq, sk, sv):
    s = sq.shape[0]
    t = SB_TILE
    assert s % t == 0
    rows = pl.BlockSpec((t, GROUP_W), lambda i: (i, 0))
    return pl.pallas_call(
        _sb_kernel,
        out_shape=jax.ShapeDtypeStruct((s, GROUP_W), BF16),
        grid=(s // t,),
        in_specs=[rows, _resident((s, GROUP_W)), _resident((s, GROUP_W))],
        out_specs=rows,
        scratch_shapes=[pltpu.VMEM((N_HEADS * t, GROUP_W), F32), pltpu.VMEM((N_HEADS * t, 1), F32)],
        compiler_params=_params("parallel"),
        name="stickbreak_attn",
    )(sq, sk, sv)


def _outproj_kernel(x_ref, yr_ref, yd_ref, ys_ref, mod_ref, ln_ref, w_ref, o_ref, *, alpha):
    vw, g = RET_VW, GROUP_W
    y = (_dot(yr_ref[...], w_ref[0:vw, :]) + _dot(yd_ref[...], w_ref[vw:vw + g, :])
         + _dot(ys_ref[...], w_ref[vw + g:vw + 2 * g, :]))
    o_ref[...] = _post_norm(x_ref[...], y, mod_ref[2:3, :], 1.0, ln_ref[0:1, :], ln_ref[1:2, :], alpha)


def _out_projection(x, y_ret, y_dil, y_sb, mod3, ln2, w_out, alpha):
    s, d = x.shape

    def rows(width):
        return pl.BlockSpec((ROW_TILE, width), lambda i: (i, 0))

    return pl.pallas_call(
        functools.partial(_outproj_kernel, alpha=alpha),
        out_shape=jax.ShapeDtypeStruct((s, d), F32),
        grid=(s // ROW_TILE,),
        in_specs=[rows(d), rows(RET_VW), rows(GROUP_W), rows(GROUP_W),
                  _resident((3, d)), _resident((2, d)), _resident(w_out.shape)],
        out_specs=rows(d),
        compiler_params=_params("parallel"),
        name="mixer_outproj",
    )(x, y_ret, y_dil, y_sb, mod3, ln2, w_out)


def kernel(x, c, w_ada, b_ada, ln_gain, ln_bias, ffn1_w_gate, ffn1_w_up, ffn1_w_down,
           w_in, w_out, ffn2_w_gate, ffn2_w_up, ffn2_w_down):
    batch, s, d = x.shape
    depth = w_ada.shape[0]
    assert batch == 1 and c.shape == (1, d)
    alpha = (2.0 * depth) ** 0.25
    half = HEAD_DIM // 2
    ret_freq = 1.0 / (ROPE_THETA ** jnp.linspace(0.0, 1.0, half, dtype=F32))
    rope_freq = 1.0 / (ROPE_THETA ** (jnp.arange(0, HEAD_DIM, 2, dtype=F32) / HEAD_DIM))
    tables = _rope_table(ret_freq, s) + _rope_table(rope_freq, s)
    mod = _modulation(c, w_ada, b_ada)
    ln = jnp.stack([ln_gain, ln_bias], axis=2)
    xs = x[0]
    for l in range(depth):
        xs = _ffn_sublayer(xs, mod[l, 0:3], ln[l, 0], ffn1_w_gate[l].astype(BF16),
                           ffn1_w_up[l].astype(BF16), ffn1_w_down[l].astype(BF16), alpha)
        rq, rk, rv, rg, dq, dk, dv, sq, sk, sv = _in_projection(xs, mod[l, 3:6], w_in[l].astype(BF16), tables)
        y_ret = _retention(rq, rk, rv, rg)
        y_dil = _dilated_attention(dq, dk, dv)
        y_sb = _stick_breaking(sq, sk, sv)
        xs = _out_projection(xs, y_ret, y_dil, y_sb, mod[l, 3:6], ln[l, 1], w_out[l].astype(BF16), alpha)
        xs = _ffn_sublayer(xs, mod[l, 6:9], ln[l, 2], ffn2_w_gate[l].astype(BF16),
                           ffn2_w_up[l].astype(BF16), ffn2_w_down[l].astype(BF16), alpha)
    return xs[None]
```

```python
import functools
import math

import jax
import jax.numpy as jnp
from jax import lax
from jax.experimental import pallas as pl
from jax.experimental.pallas import tpu as pltpu

HEAD_DIM = 64
N_HEADS = 4
RET_DV = 128
GROUP_W = N_HEADS * HEAD_DIM
RET_VW = N_HEADS * RET_DV
WINDOWS = (128, 512, 2048)
DILATIONS = (1, 4, 16)
ROPE_THETA = 10000.0
LN_EPS = 1e-5
GN_EPS = 1e-6
FFN_RES = 0.5
N_MOD = 9
LOG_GAMMA = tuple(math.log1p(-(2.0 ** (-5 - h))) for h in range(N_HEADS))

V7X_LANES = 128
N_SLABS = GROUP_W // V7X_LANES
V7X_VMEM_LIMIT_BYTES = 56 * 1024 * 1024
ROW_TILE = 512
FFN_ROW_TILE = 1024
FFN_CHUNKS = 11
RET_BLOCK = 256
DIL_TILE = 128
DIL_BLOCK = DIL_TILE * max(DILATIONS)
SB_TILE = 256
MASKED = -1e30
SB_STOP = -110.0

F32 = jnp.float32
BF16 = jnp.bfloat16


def _params(*sem):
    return pltpu.CompilerParams(dimension_semantics=sem, vmem_limit_bytes=V7X_VMEM_LIMIT_BYTES)


def _resident(shape):
    return pl.BlockSpec(shape, lambda *_: (0,) * len(shape), pipeline_mode=pl.Buffered(1))


def _layer_block(tail, layer):
    return pl.BlockSpec((None,) + tuple(tail), lambda *_: (layer,) + (0,) * len(tail), pipeline_mode=pl.Buffered(1))


def _dot(a, b):
    return jnp.dot(a, b, preferred_element_type=F32)


def _dot_nt(a, b):
    return lax.dot_general(a, b, (((1,), (1,)), ((), ())), preferred_element_type=F32)


def _silu(v):
    return v * jax.nn.sigmoid(v)


def _post_norm(x, y, gate, res_w, gain, bias, alpha):
    z = alpha * x + (res_w * (1.0 + gate)) * y
    mu = jnp.mean(z, axis=-1, keepdims=True)
    zc = z - mu
    var = jnp.mean(zc * zc, axis=-1, keepdims=True)
    return zc * lax.rsqrt(var + LN_EPS) * gain + bias


def _mod_kernel(c_ref, w_ref, b_ref, o_ref):
    c = c_ref[...]
    o_ref[...] = jnp.sum(_silu(c) * w_ref[...], axis=0, keepdims=True) + b_ref[...]


def _modulation(c, w_ada, b_ada):
    depth, d, n = w_ada.shape
    tn = d
    out = pl.pallas_call(
        _mod_kernel,
        out_shape=jax.ShapeDtypeStruct((depth, 1, n), F32),
        grid=(depth, n // tn),
        in_specs=[
            pl.BlockSpec((d, 1), lambda l, j: (0, 0)),
            pl.BlockSpec((None, d, tn), lambda l, j: (l, 0, j)),
            pl.BlockSpec((None, 1, tn), lambda l, j: (l, 0, j)),
        ],
        out_specs=pl.BlockSpec((None, 1, tn), lambda l, j: (l, 0, j)),
        compiler_params=_params("parallel", "parallel"),
        name="adaln_mod",
    )(c.reshape(d, 1), w_ada, b_ada.reshape(depth, 1, n))
    return out.reshape(depth, N_MOD, d)


def _mod_rows(mod_ref, sub):
    return tuple(mod_ref[3 * sub + i:3 * sub + i + 1, :] for i in range(3))


def _ln_rows(ln_ref, sub):
    return ln_ref[2 * sub:2 * sub + 1, :], ln_ref[2 * sub + 1:2 * sub + 2, :]


def _ffn_postnorm(x, sub, mod_ref, ln_ref, wg_ref, wu_ref, wd_ref, alpha):
    shift, scale, gate = _mod_rows(mod_ref, sub)
    d_ff = wg_ref.shape[1]
    fc = d_ff // FFN_CHUNKS
    h = (x * (1.0 + scale) + shift).astype(BF16)
    y = jnp.zeros(x.shape, F32)
    for ci in range(FFN_CHUNKS):
        cols = slice(ci * fc, (ci + 1) * fc)
        act = _silu(_dot(h, wg_ref[:, cols])) * _dot(h, wu_ref[:, cols])
        y = y + _dot(act.astype(BF16), wd_ref[cols, :])
    return _post_norm(x, y, gate, FFN_RES, *_ln_rows(ln_ref, sub), alpha)


def _ffn_kernel(x_ref, mod_ref, ln_ref, wg_ref, wu_ref, wd_ref, o_ref, *, alpha):
    o_ref[...] = _ffn_postnorm(x_ref[...], 0, mod_ref, ln_ref, wg_ref, wu_ref, wd_ref, alpha)


def _mixout_ffn_kernel(x_ref, yr_ref, yd_ref, ys_ref, mod_ref, ln_ref, wo_ref, wg_ref, wu_ref, wd_ref, o_ref,
                       *, alpha):
    vw, g = RET_VW, GROUP_W
    y = (_dot(yr_ref[...], wo_ref[0:vw, :]) + _dot(yd_ref[...], wo_ref[vw:vw + g, :])
         + _dot(ys_ref[...], wo_ref[vw + g:vw + 2 * g, :]))
    x1 = _post_norm(x_ref[...], y, _mod_rows(mod_ref, 1)[2], 1.0, *_ln_rows(ln_ref, 1), alpha)
    o_ref[...] = _ffn_postnorm(x1, 2, mod_ref, ln_ref, wg_ref, wu_ref, wd_ref, alpha)


def _ffn_specs(d, d_ff, layer):
    assert d_ff % (FFN_CHUNKS * V7X_LANES) == 0
    return [_layer_block((d, d_ff), layer), _layer_block((d, d_ff), layer), _layer_block((d_ff, d), layer)]


def _rows(tile, width):
    return pl.BlockSpec((tile, width), lambda i: (i, 0))


def _ffn_sublayer(x, layer, mod, ln, wg, wu, wd, alpha):
    s, d = x.shape
    assert s % FFN_ROW_TILE == 0
    return pl.pallas_call(
        functools.partial(_ffn_kernel, alpha=alpha),
        out_shape=jax.ShapeDtypeStruct((s, d), F32),
        grid=(s // FFN_ROW_TILE,),
        in_specs=[_rows(FFN_ROW_TILE, d), _layer_block((N_MOD, d), layer), _layer_block((6, d), layer)]
        + _ffn_specs(d, wg.shape[2], layer),
        out_specs=_rows(FFN_ROW_TILE, d),
        compiler_params=_params("parallel"),
        name="ffn_postnorm",
    )(x, mod, ln, wg, wu, wd)


def _mixout_ffn_sublayer(x, y_ret, y_dil, y_sb, layer, mod, ln, w_out, wg, wu, wd, alpha):
    s, d = x.shape
    assert s % FFN_ROW_TILE == 0
    t = FFN_ROW_TILE
    return pl.pallas_call(
        functools.partial(_mixout_ffn_kernel, alpha=alpha),
        out_shape=jax.ShapeDtypeStruct((s, d), F32),
        grid=(s // t,),
        in_specs=[_rows(t, d), _rows(t, RET_VW), _rows(t, GROUP_W), _rows(t, GROUP_W),
                  _layer_block((N_MOD, d), layer), _layer_block((6, d), layer), _layer_block(w_out.shape[1:], layer)]
        + _ffn_specs(d, wg.shape[2], layer),
        out_specs=_rows(t, d),
        compiler_params=_params("parallel"),
        name="mixout_ffn_postnorm",
    )(x, y_ret, y_dil, y_sb, mod, ln, w_out, wg, wu, wd)


def _rope(v, cos, sin):
    lane = lax.broadcasted_iota(jnp.int32, cos.shape, 1)
    first_half = (lane & (HEAD_DIM - 1)) < (HEAD_DIM // 2)
    halves = []
    for hf in range(GROUP_W // V7X_LANES):
        vh = v[:, hf * V7X_LANES:(hf + 1) * V7X_LANES]
        partner = jnp.where(first_half,
                            pltpu.roll(vh, V7X_LANES - HEAD_DIM // 2, 1),
                            pltpu.roll(vh, HEAD_DIM // 2, 1))
        halves.append(vh * cos + partner * sin)
    return jnp.concatenate(halves, axis=1)


def _inproj_kernel(x_ref, mod_ref, w_ref, cr_ref, sr_ref, cd_ref, sd_ref,
                   rq_ref, rk_ref, rv_ref, rg_ref, dq_ref, dk_ref, dv_ref, sq_ref, sk_ref, sv_ref):
    shift, scale, _ = _mod_rows(mod_ref, 1)
    h = (x_ref[...] * (1.0 + scale) + shift).astype(BF16)
    qk_scale = HEAD_DIM ** -0.5
    g, vw = GROUP_W, RET_VW

    full = _dot(h, w_ref[...])

    def proj(start, width):
        return full[:, start:start + width]

    cr, sr, cd, sd = cr_ref[...], sr_ref[...], cd_ref[...], sd_ref[...]
    rq_ref[...] = _rope(proj(0, g), cr, sr).astype(BF16)
    rk_ref[...] = (_rope(proj(g, g), cr, sr) * qk_scale).astype(BF16)
    rv_ref[...] = proj(2 * g, vw).astype(BF16)
    rg_ref[...] = proj(2 * g + vw, vw)
    base = 2 * g + 2 * vw
    for ref, val in ((dq_ref, _rope(proj(base, g), cd, sd) * qk_scale),
                     (dk_ref, _rope(proj(base + g, g), cd, sd)), (dv_ref, proj(base + 2 * g, g))):
        for sl in range(N_SLABS):
            ref[sl] = val[:, sl * V7X_LANES:(sl + 1) * V7X_LANES]
    sq_ref[...] = (proj(base + 3 * g, g) * qk_scale).astype(BF16)
    sk_ref[...] = proj(base + 4 * g, g).astype(BF16)
    sv_ref[...] = proj(base + 5 * g, g).astype(BF16)


def _in_projection(x, layer, mod, w_in, tables):
    s, d = x.shape
    n = w_in.shape[2]
    g, vw = GROUP_W, RET_VW
    assert n == 8 * g + 2 * vw

    def rows(width):
        return _rows(ROW_TILE, width)

    def flat(width, dtype):
        return jax.ShapeDtypeStruct((s, width), dtype), rows(width)

    slabs = (jax.ShapeDtypeStruct((N_SLABS, s, V7X_LANES), F32),
             pl.BlockSpec((N_SLABS, ROW_TILE, V7X_LANES), lambda i: (0, i, 0)))
    outs = (flat(g, BF16), flat(g, BF16), flat(vw, BF16), flat(vw, F32), slabs, slabs, slabs,
            flat(g, BF16), flat(g, BF16), flat(g, BF16))
    return pl.pallas_call(
        _inproj_kernel,
        out_shape=[o[0] for o in outs],
        grid=(s // ROW_TILE,),
        in_specs=[rows(d), _layer_block((N_MOD, d), layer), _layer_block((d, n), layer)] + [rows(V7X_LANES)] * 4,
        out_specs=[o[1] for o in outs],
        compiler_params=_params("parallel"),
        name="mixer_inproj",
    )(x, mod, w_in, *tables)


def _rope_table(inv_freq, s):
    half = HEAD_DIM // 2
    lane_freq = jnp.tile(inv_freq, V7X_LANES // half)
    sign = jnp.tile(jnp.concatenate([-jnp.ones(half, F32), jnp.ones(half, F32)]), V7X_LANES // HEAD_DIM)
    ang = jnp.arange(s, dtype=F32)[:, None] * lane_freq[None, :]
    return jnp.cos(ang), jnp.sin(ang) * sign[None, :]


def _head_of(idx, width):
    assert width & (width - 1) == 0
    return lax.shift_right_logical(idx, width.bit_length() - 1)


def _mod_pow2(idx, width):
    assert width & (width - 1) == 0
    return idx & (width - 1)


def _log_gamma_of(head):
    lg = jnp.full(head.shape, LOG_GAMMA[N_HEADS - 1], F32)
    for hh in range(N_HEADS - 2, -1, -1):
        lg = jnp.where(head == hh, LOG_GAMMA[hh], lg)
    return lg


def _own_head_lanes(rows):
    r = lax.broadcasted_iota(jnp.int32, (N_HEADS * rows, GROUP_W), 0)
    l = lax.broadcasted_iota(jnp.int32, (N_HEADS * rows, GROUP_W), 1)
    return _head_of(r, rows) == _head_of(l, HEAD_DIM)


def _stack_heads(q, rows):
    qf = q.astype(F32)
    return jnp.where(_own_head_lanes(rows), jnp.concatenate([qf] * N_HEADS, axis=0), 0.0).astype(q.dtype)


def _merge_heads(stacked, rows):
    l = lax.broadcasted_iota(jnp.int32, (rows, GROUP_W), 1)
    head = _head_of(l, HEAD_DIM)
    out = jnp.zeros((rows, GROUP_W), stacked.dtype)
    for hh in range(N_HEADS):
        out = jnp.where(head == hh, stacked[hh * rows:(hh + 1) * rows, :], out)
    return out


def _ret_kernel(q_ref, k_ref, v_ref, g_ref, o_ref, state_ref, decay_ref, xi_ref, zeta_ref, gc_ref):
    c = RET_BLOCK
    rows = N_HEADS * c

    @pl.when(pl.program_id(0) == 0)
    def _init():
        state_ref[...] = jnp.zeros(state_ref.shape, F32)
        r = lax.broadcasted_iota(jnp.int32, (rows, c), 0)
        j = lax.broadcasted_iota(jnp.int32, (rows, c), 1)
        diff = (_mod_pow2(r, c) - j).astype(F32)
        lg = _log_gamma_of(_head_of(r, c))
        decay_ref[...] = jnp.where(diff >= 0, jnp.exp(lg * jnp.maximum(diff, 0.0)), 0.0)
        r = lax.broadcasted_iota(jnp.int32, (rows, RET_DV), 0)
        xi_ref[...] = jnp.exp(_log_gamma_of(_head_of(r, c)) * (_mod_pow2(r, c).astype(F32) + 1.0))
        i = lax.broadcasted_iota(jnp.int32, (c, GROUP_W), 0)
        l = lax.broadcasted_iota(jnp.int32, (c, GROUP_W), 1)
        zeta_ref[...] = jnp.exp(_log_gamma_of(_head_of(l, HEAD_DIM)) * (c - 1.0 - i.astype(F32)))
        r = lax.broadcasted_iota(jnp.int32, (GROUP_W, RET_DV), 0)
        gc_ref[...] = jnp.exp(_log_gamma_of(_head_of(r, HEAD_DIM)) * float(c))

    q, k, v = q_ref[...], k_ref[...], v_ref[...]
    qs = _stack_heads(q, c)
    scores = (_dot_nt(qs, k) * decay_ref[...]).astype(BF16)
    inner = jnp.concatenate(
        [_dot(scores[hh * c:(hh + 1) * c, :], v[:, hh * RET_DV:(hh + 1) * RET_DV]) for hh in range(N_HEADS)],
        axis=0)
    state = state_ref[...]
    o = inner + _dot(qs, state.astype(BF16)) * xi_ref[...]
    mu = jnp.mean(o, axis=-1, keepdims=True)
    oc = o - mu
    var = jnp.mean(oc * oc, axis=-1, keepdims=True)
    on = oc * lax.rsqrt(var + GN_EPS)
    for hh in range(N_HEADS):
        lanes = slice(hh * RET_DV, (hh + 1) * RET_DV)
        o_ref[:, lanes] = (_silu(g_ref[:, lanes]) * on[hh * c:(hh + 1) * c, :]).astype(o_ref.dtype)
    kz_t = (k.astype(F32) * zeta_ref[...]).T.astype(BF16)
    kv = _dot(kz_t, v)
    gc = gc_ref[...]
    for hh in range(N_HEADS):
        rws = slice(hh * HEAD_DIM, (hh + 1) * HEAD_DIM)
        state_ref[rws, :] = state[rws, :] * gc[rws, :] + kv[rws, hh * RET_DV:(hh + 1) * RET_DV]


def _retention(rq, rk, rv, rg):
    s = rq.shape[0]
    c = RET_BLOCK
    assert s % c == 0

    def rows(width):
        return pl.BlockSpec((c, width), lambda i: (i, 0))

    return pl.pallas_call(
        _ret_kernel,
        out_shape=jax.ShapeDtypeStruct((s, RET_VW), BF16),
        grid=(s // c,),
        in_specs=[rows(GROUP_W), rows(GROUP_W), rows(RET_VW), rows(RET_VW)],
        out_specs=rows(RET_VW),
        scratch_shapes=[pltpu.VMEM((GROUP_W, RET_DV), F32), pltpu.VMEM((N_HEADS * c, c), F32),
                        pltpu.VMEM((N_HEADS * c, RET_DV), F32), pltpu.VMEM((c, GROUP_W), F32),
                        pltpu.VMEM((GROUP_W, RET_DV), F32)],
        compiler_params=_params("arbitrary"),
        name="retention",
    )(rq, rk, rv, rg)


def _load_rows(ref, start, rows, stride):
    idx = pl.ds(start, rows) if stride == 1 else pl.ds(start, rows, stride=stride)
    return jnp.concatenate([ref[sl, idx, :] for sl in range(N_SLABS)], axis=1)


def _store_rows(ref, start, rows, stride, val):
    idx = pl.ds(start, rows) if stride == 1 else pl.ds(start, rows, stride=stride)
    for sl in range(N_SLABS):
        ref[sl, idx, :] = val[:, sl * V7X_LANES:(sl + 1) * V7X_LANES]


def _dil_kernel(q_ref, k_ref, v_ref, o_ref, kwin_ref, vwin_ref, part_o_ref, part_l_ref, const_ref):
    t, blk = DIL_TILE, DIL_BLOCK
    b = pl.program_id(0)

    @pl.when(b == 0)
    def _first():
        kwin_ref[:, 0:blk, :] = jnp.zeros((N_SLABS, blk, V7X_LANES), F32)
        vwin_ref[:, 0:blk, :] = jnp.zeros((N_SLABS, blk, V7X_LANES), F32)

    @pl.when(b > 0)
    def _shift():
        kwin_ref[:, 0:blk, :] = kwin_ref[:, blk:2 * blk, :]
        vwin_ref[:, 0:blk, :] = vwin_ref[:, blk:2 * blk, :]

    kwin_ref[:, blk:2 * blk, :] = k_ref[...]
    vwin_ref[:, blk:2 * blk, :] = v_ref[...]

    rows = N_HEADS * t
    qi = _mod_pow2(lax.broadcasted_iota(jnp.int32, (rows, 2 * t), 0), t)
    kc = lax.broadcasted_iota(jnp.int32, (rows, 2 * t), 1)
    ahead = kc - qi
    band = jnp.where(ahead >= 0, jnp.where(ahead <= t, 0.0, MASKED), MASKED)
    const_ref[0] = jnp.where(_own_head_lanes(t), 1.0, 0.0)
    const_ref[1] = band
    const_ref[2] = jnp.where(kc >= t, band, MASKED)

    for pi, dil in enumerate(DILATIONS):

        def unit(u, carry, pi=pi, dil=dil):
            res = _mod_pow2(u, dil)
            ct = lax.shift_right_logical(u, dil.bit_length() - 1)
            q_start = res + ct * (dil * t)
            k_start = blk + q_start - dil * t
            mask_id = jnp.where(jnp.logical_and(b == 0, ct == 0), 2, 1)
            qf = _load_rows(q_ref, q_start, t, dil)
            qs = (jnp.concatenate([qf] * N_HEADS, axis=0) * const_ref[0]).astype(BF16)
            kk = _load_rows(kwin_ref, k_start, 2 * t, dil).astype(BF16)
            vv = _load_rows(vwin_ref, k_start, 2 * t, dil).astype(BF16)
            z = _dot_nt(qs, kk) + const_ref[mask_id]
            m = jnp.max(z, axis=-1, keepdims=True)
            p = jnp.exp(z - m)
            l = jnp.sum(p, axis=-1, keepdims=True)
            o = _dot(p.astype(BF16), vv) / l
            lse = jnp.broadcast_to(m + jnp.log(l), (rows, GROUP_W))
            _store_rows(part_o_ref.at[pi], q_start, t, dil, _merge_heads(o, t))
            _store_rows(part_l_ref.at[pi], q_start, t, dil, _merge_heads(lse, t))
            return carry

        lax.fori_loop(0, blk // t, unit, 0, unroll=4)

    def combine(ci, carry):
        r0 = pl.multiple_of(ci * t, t)
        for sl in range(N_SLABS):
            ls = [part_l_ref[pi, sl, pl.ds(r0, t), :] for pi in range(len(DILATIONS))]
            top = functools.reduce(jnp.maximum, ls)
            ws = [jnp.exp(x - top) for x in ls]
            num = sum(w * part_o_ref[pi, sl, pl.ds(r0, t), :] for pi, w in enumerate(ws))
            o_ref[pl.ds(r0, t), sl * V7X_LANES:(sl + 1) * V7X_LANES] = (num / sum(ws)).astype(o_ref.dtype)
        return carry

    lax.fori_loop(0, blk // t, combine, 0)


def _dilated_attention(dq, dk, dv):
    s = dq.shape[1]
    blk = DIL_BLOCK
    assert s % blk == 0 and all(w == DIL_TILE * d for w, d in zip(WINDOWS, DILATIONS))
    assert 2 * DIL_TILE == GROUP_W
    slab = pl.BlockSpec((N_SLABS, blk, V7X_LANES), lambda i: (0, i, 0))
    n_pat = len(DILATIONS)
    return pl.pallas_call(
        _dil_kernel,
        out_shape=jax.ShapeDtypeStruct((s, GROUP_W), BF16),
        grid=(s // blk,),
        in_specs=[slab, slab, slab],
        out_specs=pl.BlockSpec((blk, GROUP_W), lambda i: (i, 0)),
        scratch_shapes=[pltpu.VMEM((N_SLABS, 2 * blk, V7X_LANES), F32), pltpu.VMEM((N_SLABS, 2 * blk, V7X_LANES), F32),
                        pltpu.VMEM((n_pat, N_SLABS, blk, V7X_LANES), F32),
                        pltpu.VMEM((n_pat, N_SLABS, blk, V7X_LANES), F32),
                        pltpu.VMEM((3, N_HEADS * DIL_TILE, 2 * DIL_TILE), F32)],
        compiler_params=_params("arbitrary"),
        name="dilated_attn",
    )(dq, dk, dv)


def _sb_kernel(q_ref, k_ref, v_ref, o_ref, acc_ref, carry_ref):
    t = SB_TILE
    rows = N_HEADS * t
    qi = pl.program_id(0)
    qs = _stack_heads(q_ref[...], t)
    acc_ref[...] = jnp.zeros(acc_ref.shape, F32)
    carry_ref[...] = jnp.zeros(carry_ref.shape, F32)
    r = lax.broadcasted_iota(jnp.int32, (rows, t), 0)
    col = lax.broadcasted_iota(jnp.int32, (rows, t), 1)
    causal = _mod_pow2(r, t) > col
    kj = lax.broadcasted_iota(jnp.int32, (t, t), 0)
    ks = lax.broadcasted_iota(jnp.int32, (t, t), 1)
    later = jnp.where(kj > ks, 1.0, 0.0).astype(BF16)

    def unfinished(state):
        jj, carry_max = state
        return jnp.logical_and(jj <= qi, carry_max > SB_STOP)

    def visit(jj, on_diagonal):
        k0 = pl.multiple_of((qi - jj) * t, t)
        z = _dot_nt(qs, k_ref[pl.ds(k0, t), :])
        softplus = jnp.maximum(z, 0.0) + jnp.log(1.0 + jnp.exp(-jnp.abs(z)))
        log_keep = -softplus
        if on_diagonal:
            log_keep = jnp.where(causal, log_keep, 0.0)
        within = _dot(log_keep.astype(BF16), later)
        a = jnp.exp((z - softplus) + within + carry_ref[...])
        if on_diagonal:
            a = jnp.where(causal, a, 0.0)
        acc_ref[...] += _dot(a.astype(BF16), v_ref[pl.ds(k0, t), :])
        carry = carry_ref[...] + jnp.sum(log_keep, axis=-1, keepdims=True)
        carry_ref[...] = carry
        return jnp.max(carry)

    lax.while_loop(unfinished, lambda st: (st[0] + 1, visit(st[0], False)), (jnp.int32(1), visit(0, True)))
    o_ref[...] = _merge_heads(acc_ref[...], t).astype(o_ref.dtype)


def _stick_breaking(sq, sk, sv):
    s = sq.shape[0]
    t = SB_TILE
    assert s % t == 0
    rows = pl.BlockSpec((t, GROUP_W), lambda i: (i, 0))
    return pl.pallas_call(
        _sb_kernel,
        out_shape=jax.ShapeDtypeStruct((s, GROUP_W), BF16),
        grid=(s // t,),
        in_specs=[rows, _resident((s, GROUP_W)), _resident((s, GROUP_W))],
        out_specs=rows,
        scratch_shapes=[pltpu.VMEM((N_HEADS * t, GROUP_W), F32), pltpu.VMEM((N_HEADS * t, 1), F32)],
        compiler_params=_params("parallel"),
        name="stickbreak_attn",
    )(sq, sk, sv)


def kernel(x, c, w_ada, b_ada, ln_gain, ln_bias, ffn1_w_gate, ffn1_w_up, ffn1_w_down,
           w_in, w_out, ffn2_w_gate, ffn2_w_up, ffn2_w_down):
    batch, s, d = x.shape
    depth = w_ada.shape[0]
    assert batch == 1 and c.shape == (1, d)
    alpha = (2.0 * depth) ** 0.25
    half = HEAD_DIM // 2
    ret_freq = 1.0 / (ROPE_THETA ** jnp.linspace(0.0, 1.0, half, dtype=F32))
    rope_freq = 1.0 / (ROPE_THETA ** (jnp.arange(0, HEAD_DIM, 2, dtype=F32) / HEAD_DIM))
    tables = _rope_table(ret_freq, s) + _rope_table(rope_freq, s)
    mod = _modulation(c, w_ada, b_ada)
    ln = jnp.stack([ln_gain, ln_bias], axis=2).reshape(depth, 6, d)
    f1g, f1u, f1d, win, wout, f2g, f2u, f2d = (
        w.astype(BF16) for w in (ffn1_w_gate, ffn1_w_up, ffn1_w_down, w_in, w_out, ffn2_w_gate, ffn2_w_up, ffn2_w_down))
    xs = x[0]
    for l in range(depth):
        xs = _ffn_sublayer(xs, l, mod, ln, f1g, f1u, f1d, alpha)
        rq, rk, rv, rg, dq, dk, dv, sq, sk, sv = _in_projection(xs, l, mod, win, tables)
        y_ret = _retention(rq, rk, rv, rg)
        y_dil = _dilated_attention(dq, dk, dv)
        y_sb = _stick_breaking(sq, sk, sv)
        xs = _mixout_ffn_sublayer(xs, y_ret, y_dil, y_sb, l, mod, ln, wout, f2g, f2u, f2d, alpha)
    return xs[None]
```

```python
import functools
import math

import jax
import jax.numpy as jnp
from jax import lax
from jax.experimental import pallas as pl
from jax.experimental.pallas import tpu as pltpu

HEAD_DIM = 64
N_HEADS = 4
RET_DV = 128
GROUP_W = N_HEADS * HEAD_DIM
RET_VW = N_HEADS * RET_DV
WINDOWS = (128, 512, 2048)
DILATIONS = (1, 4, 16)
ROPE_THETA = 10000.0
LN_EPS = 1e-5
GN_EPS = 1e-6
FFN_RES = 0.5
N_MOD = 9
LOG_GAMMA = tuple(math.log1p(-(2.0 ** (-5 - h))) for h in range(N_HEADS))

V7X_LANES = 128
N_SLABS = GROUP_W // V7X_LANES
V7X_VMEM_LIMIT_BYTES = 56 * 1024 * 1024
ROW_TILE = 512
FFN_ROW_TILE = 1024
FFN_CHUNKS = 11
RET_BLOCK = 256
RET_CHUNKS_PER_STEP = 2
DIL_TILE = 128
DIL_BLOCK = DIL_TILE * max(DILATIONS)
SB_TILE = 256
MASKED = -1e30
SB_STOP = -110.0

F32 = jnp.float32
BF16 = jnp.bfloat16


def _params(*sem):
    return pltpu.CompilerParams(dimension_semantics=sem, vmem_limit_bytes=V7X_VMEM_LIMIT_BYTES)


def _resident(shape):
    return pl.BlockSpec(shape, lambda *_: (0,) * len(shape), pipeline_mode=pl.Buffered(1))


def _layer_block(tail, layer):
    return pl.BlockSpec((None,) + tuple(tail), lambda *_: (layer,) + (0,) * len(tail), pipeline_mode=pl.Buffered(1))


def _dot(a, b):
    return jnp.dot(a, b, preferred_element_type=F32)


def _dot_nt(a, b):
    return lax.dot_general(a, b, (((1,), (1,)), ((), ())), preferred_element_type=F32)


def _silu(v):
    return v * jax.nn.sigmoid(v)


def _post_norm(x, y, gate, res_w, gain, bias, alpha):
    z = alpha * x + (res_w * (1.0 + gate)) * y
    mu = jnp.mean(z, axis=-1, keepdims=True)
    zc = z - mu
    var = jnp.mean(zc * zc, axis=-1, keepdims=True)
    return zc * lax.rsqrt(var + LN_EPS) * gain + bias


def _mod_kernel(c_ref, w_ref, b_ref, o_ref):
    c = c_ref[...]
    o_ref[...] = jnp.sum(_silu(c) * w_ref[...], axis=0, keepdims=True) + b_ref[...]


def _modulation(c, w_ada, b_ada):
    depth, d, n = w_ada.shape
    tn = d
    out = pl.pallas_call(
        _mod_kernel,
        out_shape=jax.ShapeDtypeStruct((depth, 1, n), F32),
        grid=(depth, n // tn),
        in_specs=[
            pl.BlockSpec((d, 1), lambda l, j: (0, 0)),
            pl.BlockSpec((None, d, tn), lambda l, j: (l, 0, j)),
            pl.BlockSpec((None, 1, tn), lambda l, j: (l, 0, j)),
        ],
        out_specs=pl.BlockSpec((None, 1, tn), lambda l, j: (l, 0, j)),
        compiler_params=_params("parallel", "parallel"),
        name="adaln_mod",
    )(c.reshape(d, 1), w_ada, b_ada.reshape(depth, 1, n))
    return out.reshape(depth, N_MOD, d)


def _mod_rows(mod_ref, sub):
    return tuple(mod_ref[3 * sub + i:3 * sub + i + 1, :] for i in range(3))


def _ln_rows(ln_ref, sub):
    return ln_ref[2 * sub:2 * sub + 1, :], ln_ref[2 * sub + 1:2 * sub + 2, :]


def _ffn_postnorm(x, sub, mod_ref, ln_ref, wg_ref, wu_ref, wd_ref, alpha):
    shift, scale, gate = _mod_rows(mod_ref, sub)
    d_ff = wg_ref.shape[1]
    fc = d_ff // FFN_CHUNKS
    h = (x * (1.0 + scale) + shift).astype(BF16)
    y = jnp.zeros(x.shape, F32)
    for ci in range(FFN_CHUNKS):
        cols = slice(ci * fc, (ci + 1) * fc)
        act = _silu(_dot(h, wg_ref[:, cols])) * _dot(h, wu_ref[:, cols])
        y = y + _dot(act.astype(BF16), wd_ref[cols, :])
    return _post_norm(x, y, gate, FFN_RES, *_ln_rows(ln_ref, sub), alpha)


def _ffn_kernel(x_ref, mod_ref, ln_ref, wg_ref, wu_ref, wd_ref, o_ref, *, alpha):
    o_ref[...] = _ffn_postnorm(x_ref[...], 0, mod_ref, ln_ref, wg_ref, wu_ref, wd_ref, alpha)


def _mixout_ffn_kernel(x_ref, yr_ref, yd_ref, ys_ref, mod_ref, ln_ref, wo_ref, wg_ref, wu_ref, wd_ref, o_ref,
                       *, alpha):
    vw, g = RET_VW, GROUP_W
    y = (_dot(yr_ref[...], wo_ref[0:vw, :]) + _dot(yd_ref[...], wo_ref[vw:vw + g, :])
         + _dot(ys_ref[...], wo_ref[vw + g:vw + 2 * g, :]))
    x1 = _post_norm(x_ref[...], y, _mod_rows(mod_ref, 1)[2], 1.0, *_ln_rows(ln_ref, 1), alpha)
    o_ref[...] = _ffn_postnorm(x1, 2, mod_ref, ln_ref, wg_ref, wu_ref, wd_ref, alpha)


def _ffn_specs(d, d_ff, layer):
    assert d_ff % (FFN_CHUNKS * V7X_LANES) == 0
    return [_layer_block((d, d_ff), layer), _layer_block((d, d_ff), layer), _layer_block((d_ff, d), layer)]


def _rows(tile, width):
    return pl.BlockSpec((tile, width), lambda i: (i, 0))


def _ffn_sublayer(x, layer, mod, ln, wg, wu, wd, alpha):
    s, d = x.shape
    assert s % FFN_ROW_TILE == 0
    return pl.pallas_call(
        functools.partial(_ffn_kernel, alpha=alpha),
        out_shape=jax.ShapeDtypeStruct((s, d), F32),
        grid=(s // FFN_ROW_TILE,),
        in_specs=[_rows(FFN_ROW_TILE, d), _layer_block((N_MOD, d), layer), _layer_block((6, d), layer)]
        + _ffn_specs(d, wg.shape[2], layer),
        out_specs=_rows(FFN_ROW_TILE, d),
        compiler_params=_params("parallel"),
        name="ffn_postnorm",
    )(x, mod, ln, wg, wu, wd)


def _mixout_ffn_sublayer(x, y_ret, y_dil, y_sb, layer, mod, ln, w_out, wg, wu, wd, alpha):
    s, d = x.shape
    assert s % FFN_ROW_TILE == 0
    t = FFN_ROW_TILE
    return pl.pallas_call(
        functools.partial(_mixout_ffn_kernel, alpha=alpha),
        out_shape=jax.ShapeDtypeStruct((s, d), F32),
        grid=(s // t,),
        in_specs=[_rows(t, d), _rows(t, RET_VW), _rows(t, GROUP_W), _rows(t, GROUP_W),
                  _layer_block((N_MOD, d), layer), _layer_block((6, d), layer), _layer_block(w_out.shape[1:], layer)]
        + _ffn_specs(d, wg.shape[2], layer),
        out_specs=_rows(t, d),
        compiler_params=_params("parallel"),
        name="mixout_ffn_postnorm",
    )(x, y_ret, y_dil, y_sb, mod, ln, w_out, wg, wu, wd)


def _rope(v, cos, sin):
    lane = lax.broadcasted_iota(jnp.int32, cos.shape, 1)
    first_half = (lane & (HEAD_DIM - 1)) < (HEAD_DIM // 2)
    halves = []
    for hf in range(GROUP_W // V7X_LANES):
        vh = v[:, hf * V7X_LANES:(hf + 1) * V7X_LANES]
        partner = jnp.where(first_half,
                            pltpu.roll(vh, V7X_LANES - HEAD_DIM // 2, 1),
                            pltpu.roll(vh, HEAD_DIM // 2, 1))
        halves.append(vh * cos + partner * sin)
    return jnp.concatenate(halves, axis=1)


def _tile_rope(within_ref, start_ref):
    cos_r, sin_r = within_ref[0], within_ref[1]
    c0, s0, c0s, s0s = (start_ref[i:i + 1, :] for i in range(4))
    return c0 * cos_r - s0 * sin_r, s0s * cos_r + c0s * sin_r


def _inproj_kernel(x_ref, mod_ref, w_ref, rw_ref, rs_ref, dw_ref, ds_ref,
                   rq_ref, rk_ref, rv_ref, rg_ref, dq_ref, dk_ref, dv_ref, sq_ref, sk_ref, sv_ref):
    shift, scale, _ = _mod_rows(mod_ref, 1)
    h = (x_ref[...] * (1.0 + scale) + shift).astype(BF16)
    qk_scale = HEAD_DIM ** -0.5
    g, vw = GROUP_W, RET_VW

    full = _dot(h, w_ref[...])

    def proj(start, width):
        return full[:, start:start + width]

    cr, sr = _tile_rope(rw_ref, rs_ref)
    cd, sd = _tile_rope(dw_ref, ds_ref)
    rq_ref[...] = _rope(proj(0, g), cr, sr).astype(BF16)
    rk_ref[...] = (_rope(proj(g, g), cr, sr) * qk_scale).astype(BF16)
    rv_ref[...] = proj(2 * g, vw).astype(BF16)
    rg_ref[...] = proj(2 * g + vw, vw)
    base = 2 * g + 2 * vw
    for ref, val in ((dq_ref, _rope(proj(base, g), cd, sd) * qk_scale),
                     (dk_ref, _rope(proj(base + g, g), cd, sd)), (dv_ref, proj(base + 2 * g, g))):
        for sl in range(N_SLABS):
            ref[sl] = val[:, sl * V7X_LANES:(sl + 1) * V7X_LANES]
    sq_ref[...] = (proj(base + 3 * g, g) * qk_scale).astype(BF16)
    sk_ref[...] = proj(base + 4 * g, g).astype(BF16)
    sv_ref[...] = proj(base + 5 * g, g).astype(BF16)


def _in_projection(x, layer, mod, w_in, tables):
    s, d = x.shape
    n = w_in.shape[2]
    g, vw = GROUP_W, RET_VW
    assert n == 8 * g + 2 * vw

    def rows(width):
        return _rows(ROW_TILE, width)

    def flat(width, dtype):
        return jax.ShapeDtypeStruct((s, width), dtype), rows(width)

    slabs = (jax.ShapeDtypeStruct((N_SLABS, s, V7X_LANES), F32),
             pl.BlockSpec((N_SLABS, ROW_TILE, V7X_LANES), lambda i: (0, i, 0)))
    outs = (flat(g, BF16), flat(g, BF16), flat(vw, BF16), flat(vw, F32), slabs, slabs, slabs,
            flat(g, BF16), flat(g, BF16), flat(g, BF16))
    return pl.pallas_call(
        _inproj_kernel,
        out_shape=[o[0] for o in outs],
        grid=(s // ROW_TILE,),
        in_specs=[rows(d), _layer_block((N_MOD, d), layer), _layer_block((d, n), layer)]
        + [_resident((2, ROW_TILE, V7X_LANES)), pl.BlockSpec((None, 4, V7X_LANES), lambda i: (i, 0, 0))] * 2,
        out_specs=[o[1] for o in outs],
        compiler_params=_params("parallel"),
        name="mixer_inproj",
    )(x, mod, w_in, *tables)


def _rope_tables(inv_freq, s):
    half = HEAD_DIM // 2
    lane_freq = jnp.tile(inv_freq, V7X_LANES // half)[None, :]
    sign = jnp.tile(jnp.concatenate([-jnp.ones(half, F32), jnp.ones(half, F32)]), V7X_LANES // HEAD_DIM)[None, :]
    within = jnp.arange(ROW_TILE, dtype=F32)[:, None] * lane_freq
    start = (jnp.arange(s // ROW_TILE, dtype=F32) * ROW_TILE)[:, None] * lane_freq
    c0, s0 = jnp.cos(start), jnp.sin(start)
    return (jnp.stack([jnp.cos(within), jnp.sin(within)]), jnp.stack([c0, s0, sign * c0, sign * s0], axis=1))


def _head_of(idx, width):
    assert width & (width - 1) == 0
    return lax.shift_right_logical(idx, width.bit_length() - 1)


def _mod_pow2(idx, width):
    assert width & (width - 1) == 0
    return idx & (width - 1)


def _log_gamma_of(head):
    lg = jnp.full(head.shape, LOG_GAMMA[N_HEADS - 1], F32)
    for hh in range(N_HEADS - 2, -1, -1):
        lg = jnp.where(head == hh, LOG_GAMMA[hh], lg)
    return lg


def _own_head_lanes(rows):
    r = lax.broadcasted_iota(jnp.int32, (N_HEADS * rows, GROUP_W), 0)
    l = lax.broadcasted_iota(jnp.int32, (N_HEADS * rows, GROUP_W), 1)
    return _head_of(r, rows) == _head_of(l, HEAD_DIM)


def _stack_heads(q, rows):
    qf = q.astype(F32)
    return jnp.where(_own_head_lanes(rows), jnp.concatenate([qf] * N_HEADS, axis=0), 0.0).astype(q.dtype)


def _merge_heads(stacked, rows):
    l = lax.broadcasted_iota(jnp.int32, (rows, GROUP_W), 1)
    head = _head_of(l, HEAD_DIM)
    out = jnp.zeros((rows, GROUP_W), stacked.dtype)
    for hh in range(N_HEADS):
        out = jnp.where(head == hh, stacked[hh * rows:(hh + 1) * rows, :], out)
    return out


def _ret_kernel(q_ref, k_ref, v_ref, g_ref, o_ref, state_ref, decay_ref, xi_ref, zeta_ref, gc_ref):
    c = RET_BLOCK
    rows = N_HEADS * c

    @pl.when(pl.program_id(0) == 0)
    def _init():
        state_ref[...] = jnp.zeros(state_ref.shape, F32)
        r = lax.broadcasted_iota(jnp.int32, (rows, c), 0)
        j = lax.broadcasted_iota(jnp.int32, (rows, c), 1)
        diff = (_mod_pow2(r, c) - j).astype(F32)
        lg = _log_gamma_of(_head_of(r, c))
        decay_ref[...] = jnp.where(diff >= 0, jnp.exp(lg * jnp.maximum(diff, 0.0)), 0.0)
        r = lax.broadcasted_iota(jnp.int32, (rows, RET_DV), 0)
        xi_ref[...] = jnp.exp(_log_gamma_of(_head_of(r, c)) * (_mod_pow2(r, c).astype(F32) + 1.0))
        i = lax.broadcasted_iota(jnp.int32, (c, GROUP_W), 0)
        l = lax.broadcasted_iota(jnp.int32, (c, GROUP_W), 1)
        zeta_ref[...] = jnp.exp(_log_gamma_of(_head_of(l, HEAD_DIM)) * (c - 1.0 - i.astype(F32)))
        r = lax.broadcasted_iota(jnp.int32, (GROUP_W, RET_DV), 0)
        gc_ref[...] = jnp.exp(_log_gamma_of(_head_of(r, HEAD_DIM)) * float(c))

    state = state_ref[...]
    gc = gc_ref[...]
    for ci in range(RET_CHUNKS_PER_STEP):
        rsl = slice(ci * c, (ci + 1) * c)
        q, k, v = q_ref[rsl, :], k_ref[rsl, :], v_ref[rsl, :]
        qs = _stack_heads(q, c)
        scores = (_dot_nt(qs, k) * decay_ref[...]).astype(BF16)
        inner = jnp.concatenate(
            [_dot(scores[hh * c:(hh + 1) * c, :], v[:, hh * RET_DV:(hh + 1) * RET_DV]) for hh in range(N_HEADS)],
            axis=0)
        o = inner + _dot(qs, state.astype(BF16)) * xi_ref[...]
        mu = jnp.mean(o, axis=-1, keepdims=True)
        oc = o - mu
        var = jnp.mean(oc * oc, axis=-1, keepdims=True)
        on = oc * lax.rsqrt(var + GN_EPS)
        for hh in range(N_HEADS):
            lanes = slice(hh * RET_DV, (hh + 1) * RET_DV)
            o_ref[rsl, lanes] = (_silu(g_ref[rsl, lanes]) * on[hh * c:(hh + 1) * c, :]).astype(o_ref.dtype)
        kz_t = (k.astype(F32) * zeta_ref[...]).T.astype(BF16)
        kv = _dot(kz_t, v)
        state = jnp.concatenate(
            [state[hh * HEAD_DIM:(hh + 1) * HEAD_DIM, :] * gc[hh * HEAD_DIM:(hh + 1) * HEAD_DIM, :]
             + kv[hh * HEAD_DIM:(hh + 1) * HEAD_DIM, hh * RET_DV:(hh + 1) * RET_DV] for hh in range(N_HEADS)], axis=0)
    state_ref[...] = state


def _retention(rq, rk, rv, rg):
    s = rq.shape[0]
    c = RET_BLOCK
    step_rows = c * RET_CHUNKS_PER_STEP
    assert s % step_rows == 0

    def rows(width):
        return _rows(step_rows, width)

    return pl.pallas_call(
        _ret_kernel,
        out_shape=jax.ShapeDtypeStruct((s, RET_VW), BF16),
        grid=(s // step_rows,),
        in_specs=[rows(GROUP_W), rows(GROUP_W), rows(RET_VW), rows(RET_VW)],
        out_specs=rows(RET_VW),
        scratch_shapes=[pltpu.VMEM((GROUP_W, RET_DV), F32), pltpu.VMEM((N_HEADS * c, c), F32),
                        pltpu.VMEM((N_HEADS * c, RET_DV), F32), pltpu.VMEM((c, GROUP_W), F32),
                        pltpu.VMEM((GROUP_W, RET_DV), F32)],
        compiler_params=_params("arbitrary"),
        name="retention",
    )(rq, rk, rv, rg)


def _load_rows(ref, start, rows, stride):
    idx = pl.ds(start, rows) if stride == 1 else pl.ds(start, rows, stride=stride)
    return jnp.concatenate([ref[sl, idx, :] for sl in range(N_SLABS)], axis=1)


def _store_rows(ref, start, rows, stride, val):
    idx = pl.ds(start, rows) if stride == 1 else pl.ds(start, rows, stride=stride)
    for sl in range(N_SLABS):
        ref[sl, idx, :] = val[:, sl * V7X_LANES:(sl + 1) * V7X_LANES]


def _dil_kernel(q_ref, k_ref, v_ref, o_ref, kwin_ref, vwin_ref, part_o_ref, part_l_ref, const_ref):
    t, blk = DIL_TILE, DIL_BLOCK
    b = pl.program_id(0)

    @pl.when(b == 0)
    def _first():
        kwin_ref[:, 0:blk, :] = jnp.zeros((N_SLABS, blk, V7X_LANES), F32)
        vwin_ref[:, 0:blk, :] = jnp.zeros((N_SLABS, blk, V7X_LANES), F32)

    @pl.when(b > 0)
    def _shift():
        kwin_ref[:, 0:blk, :] = kwin_ref[:, blk:2 * blk, :]
        vwin_ref[:, 0:blk, :] = vwin_ref[:, blk:2 * blk, :]

    kwin_ref[:, blk:2 * blk, :] = k_ref[...]
    vwin_ref[:, blk:2 * blk, :] = v_ref[...]

    rows = N_HEADS * t
    qi = _mod_pow2(lax.broadcasted_iota(jnp.int32, (rows, 2 * t), 0), t)
    kc = lax.broadcasted_iota(jnp.int32, (rows, 2 * t), 1)
    ahead = kc - qi
    band = jnp.where(ahead >= 0, jnp.where(ahead <= t, 0.0, MASKED), MASKED)
    const_ref[0] = jnp.where(_own_head_lanes(t), 1.0, 0.0)
    const_ref[1] = band
    const_ref[2] = jnp.where(kc >= t, band, MASKED)

    for pi, dil in enumerate(DILATIONS):

        def unit(u, carry, pi=pi, dil=dil):
            res = _mod_pow2(u, dil)
            ct = lax.shift_right_logical(u, dil.bit_length() - 1)
            q_start = res + ct * (dil * t)
            k_start = blk + q_start - dil * t
            mask_id = jnp.where(jnp.logical_and(b == 0, ct == 0), 2, 1)
            qf = _load_rows(q_ref, q_start, t, dil)
            qs = (jnp.concatenate([qf] * N_HEADS, axis=0) * const_ref[0]).astype(BF16)
            kk = _load_rows(kwin_ref, k_start, 2 * t, dil).astype(BF16)
            vv = _load_rows(vwin_ref, k_start, 2 * t, dil).astype(BF16)
            z = _dot_nt(qs, kk) + const_ref[mask_id]
            m = jnp.max(z, axis=-1, keepdims=True)
            p = jnp.exp(z - m)
            l = jnp.sum(p, axis=-1, keepdims=True)
            o = _dot(p.astype(BF16), vv) / l
            lse = jnp.broadcast_to(m + jnp.log(l), (rows, GROUP_W))
            _store_rows(part_o_ref.at[pi], q_start, t, dil, _merge_heads(o, t))
            _store_rows(part_l_ref.at[pi], q_start, t, dil, _merge_heads(lse, t))
            return carry

        lax.fori_loop(0, blk // t, unit, 0, unroll=4)

    def combine(ci, carry):
        r0 = pl.multiple_of(ci * t, t)
        for sl in range(N_SLABS):
            ls = [part_l_ref[pi, sl, pl.ds(r0, t), :] for pi in range(len(DILATIONS))]
            top = functools.reduce(jnp.maximum, ls)
            ws = [jnp.exp(x - top) for x in ls]
            num = sum(w * part_o_ref[pi, sl, pl.ds(r0, t), :] for pi, w in enumerate(ws))
            o_ref[pl.ds(r0, t), sl * V7X_LANES:(sl + 1) * V7X_LANES] = (num / sum(ws)).astype(o_ref.dtype)
        return carry

    lax.fori_loop(0, blk // t, combine, 0)


def _dilated_attention(dq, dk, dv):
    s = dq.shape[1]
    blk = DIL_BLOCK
    assert s % blk == 0 and all(w == DIL_TILE * d for w, d in zip(WINDOWS, DILATIONS))
    assert 2 * DIL_TILE == GROUP_W
    slab = pl.BlockSpec((N_SLABS, blk, V7X_LANES), lambda i: (0, i, 0))
    n_pat = len(DILATIONS)
    return pl.pallas_call(
        _dil_kernel,
        out_shape=jax.ShapeDtypeStruct((s, GROUP_W), BF16),
        grid=(s // blk,),
        in_specs=[slab, slab, slab],
        out_specs=pl.BlockSpec((blk, GROUP_W), lambda i: (i, 0)),
        scratch_shapes=[pltpu.VMEM((N_SLABS, 2 * blk, V7X_LANES), F32), pltpu.VMEM((N_SLABS, 2 * blk, V7X_LANES), F32),
                        pltpu.VMEM((n_pat, N_SLABS, blk, V7X_LANES), F32),
                        pltpu.VMEM((n_pat, N_SLABS, blk, V7X_LANES), F32),
                        pltpu.VMEM((3, N_HEADS * DIL_TILE, 2 * DIL_TILE), F32)],
        compiler_params=_params("arbitrary"),
        name="dilated_attn",
    )(dq, dk, dv)


def _sb_kernel(q_ref, k_ref, v_ref, o_ref, acc_ref, carry_ref):
    t = SB_TILE
    rows = N_HEADS * t
    qi = pl.program_id(0)
    qs = _stack_heads(q_ref[...], t)
    r = lax.broadcasted_iota(jnp.int32, (rows, t), 0)
    col = lax.broadcasted_iota(jnp.int32, (rows, t), 1)
    causal = _mod_pow2(r, t) > col
    kj = lax.broadcasted_iota(jnp.int32, (t, t), 0)
    ks = lax.broadcasted_iota(jnp.int32, (t, t), 1)
    later = jnp.where(kj > ks, 1.0, 0.0).astype(BF16)

    def visit(jj, carry, on_diagonal=False, live=None):
        k0 = pl.multiple_of(jnp.maximum(qi - jj, 0) * t, t)
        z = _dot_nt(qs, k_ref[pl.ds(k0, t), :])
        softplus = jnp.maximum(z, 0.0) + jnp.log(1.0 + jnp.exp(-jnp.abs(z)))
        log_keep = -softplus
        if on_diagonal:
            log_keep = jnp.where(causal, log_keep, 0.0)
        if live is not None:
            log_keep = log_keep * live
        within = _dot(log_keep.astype(BF16), later)
        a = jnp.exp((z - softplus) + within + carry)
        if on_diagonal:
            a = jnp.where(causal, a, 0.0)
        if live is not None:
            a = a * live
        return _dot(a.astype(BF16), v_ref[pl.ds(k0, t), :]), carry + jnp.sum(log_keep, axis=-1, keepdims=True)

    out0, carry0 = visit(0, jnp.zeros((rows, 1), F32), on_diagonal=True)
    out1, carry1 = visit(1, carry0, live=jnp.where(qi >= 1, 1.0, 0.0))
    acc_ref[...] = out0 + out1
    carry_ref[...] = carry1

    def unfinished(state):
        jj, carry_max = state
        return jnp.logical_and(jj <= qi, carry_max > SB_STOP)

    def sweep(state):
        jj, _ = state
        out, carry = visit(jj, carry_ref[...])
        acc_ref[...] += out
        carry_ref[...] = carry
        return jj + 1, jnp.max(carry)

    lax.while_loop(unfinished, sweep, (jnp.int32(2), jnp.max(carry1)))
    o_ref[...] = _merge_heads(acc_ref[...], t).astype(o_ref.dtype)


def _stick_breaking(sq, sk, sv):
    s = sq.shape[0]
    t = SB_TILE
    assert s % t == 0
    rows = pl.BlockSpec((t, GROUP_W), lambda i: (i, 0))
    return pl.pallas_call(
        _sb_kernel,
        out_shape=jax.ShapeDtypeStruct((s, GROUP_W), BF16),
        grid=(s // t,),
        in_specs=[rows, _resident((s, GROUP_W)), _resident((s, GROUP_W))],
        out_specs=rows,
        scratch_shapes=[pltpu.VMEM((N_HEADS * t, GROUP_W), F32), pltpu.VMEM((N_HEADS * t, 1), F32)],
        compiler_params=_params("parallel"),
        name="stickbreak_attn",
    )(sq, sk, sv)


def kernel(x, c, w_ada, b_ada, ln_gain, ln_bias, ffn1_w_gate, ffn1_w_up, ffn1_w_down,
           w_in, w_out, ffn2_w_gate, ffn2_w_up, ffn2_w_down):
    batch, s, d = x.shape
    depth = w_ada.shape[0]
    assert batch == 1 and c.shape == (1, d)
    alpha = (2.0 * depth) ** 0.25
    half = HEAD_DIM // 2
    ret_freq = 1.0 / (ROPE_THETA ** jnp.linspace(0.0, 1.0, half, dtype=F32))
    rope_freq = 1.0 / (ROPE_THETA ** (jnp.arange(0, HEAD_DIM, 2, dtype=F32) / HEAD_DIM))
    tables = _rope_tables(ret_freq, s) + _rope_tables(rope_freq, s)
    mod = _modulation(c, w_ada, b_ada)
    ln = jnp.stack([ln_gain, ln_bias], axis=2).reshape(depth, 6, d)
    f1g, f1u, f1d, win, wout, f2g, f2u, f2d = (
        w.astype(BF16) for w in (ffn1_w_gate, ffn1_w_up, ffn1_w_down, w_in, w_out, ffn2_w_gate, ffn2_w_up, ffn2_w_down))
    xs = x[0]
    for l in range(depth):
        xs = _ffn_sublayer(xs, l, mod, ln, f1g, f1u, f1d, alpha)
        rq, rk, rv, rg, dq, dk, dv, sq, sk, sv = _in_projection(xs, l, mod, win, tables)
        y_ret = _retention(rq, rk, rv, rg)
        y_dil = _dilated_attention(dq, dk, dv)
        y_sb = _stick_breaking(sq, sk, sv)
        xs = _mixout_ffn_sublayer(xs, y_ret, y_dil, y_sb, l, mod, ln, wout, f2g, f2u, f2d, alpha)
    return xs[None]
```

```python
import functools
import math

import jax
import jax.numpy as jnp
from jax import lax
from jax.experimental import pallas as pl
from jax.experimental.pallas import tpu as pltpu

HEAD_DIM = 64
N_HEADS = 4
RET_DV = 128
GROUP_W = N_HEADS * HEAD_DIM
RET_VW = N_HEADS * RET_DV
WINDOWS = (128, 512, 2048)
DILATIONS = (1, 4, 16)
ROPE_THETA = 10000.0
LN_EPS = 1e-5
GN_EPS = 1e-6
FFN_RES = 0.5
N_MOD = 9
LOG_GAMMA = tuple(math.log1p(-(2.0 ** (-5 - h))) for h in range(N_HEADS))

V7X_LANES = 128
N_SLABS = GROUP_W // V7X_LANES
V7X_VMEM_LIMIT_BYTES = 56 * 1024 * 1024
ROW_TILE = 512
FFN_ROW_TILE = 1024
FFN_CHUNKS = 11
RET_BLOCK = 256
RET_CHUNKS_PER_STEP = 4
DIL_TILE = 128
DIL_BLOCK = DIL_TILE * max(DILATIONS)
SB_TILE = 256
SB_TILES_PER_STEP = 2
MASKED = -1e30
SB_STOP = -110.0

F32 = jnp.float32
BF16 = jnp.bfloat16


def _params(*sem):
    return pltpu.CompilerParams(dimension_semantics=sem, vmem_limit_bytes=V7X_VMEM_LIMIT_BYTES)


def _resident(shape):
    return pl.BlockSpec(shape, lambda *_: (0,) * len(shape), pipeline_mode=pl.Buffered(1))


def _layer_block(tail, layer):
    return pl.BlockSpec((None,) + tuple(tail), lambda *_: (layer,) + (0,) * len(tail), pipeline_mode=pl.Buffered(1))


def _dot(a, b):
    return jnp.dot(a, b, preferred_element_type=F32)


def _dot_nt(a, b):
    return lax.dot_general(a, b, (((1,), (1,)), ((), ())), preferred_element_type=F32)


def _silu(v):
    return v * jax.nn.sigmoid(v)


def _post_norm(x, y, gate, res_w, gain, bias, alpha):
    z = alpha * x + (res_w * (1.0 + gate)) * y
    mu = jnp.mean(z, axis=-1, keepdims=True)
    zc = z - mu
    var = jnp.mean(zc * zc, axis=-1, keepdims=True)
    return zc * lax.rsqrt(var + LN_EPS) * gain + bias


def _mod_kernel(c_ref, w_ref, b_ref, o_ref):
    c = c_ref[...]
    o_ref[...] = jnp.sum(_silu(c) * w_ref[...], axis=0, keepdims=True) + b_ref[...]


def _modulation(c, w_ada, b_ada):
    depth, d, n = w_ada.shape
    tn = d
    out = pl.pallas_call(
        _mod_kernel,
        out_shape=jax.ShapeDtypeStruct((depth, 1, n), F32),
        grid=(depth, n // tn),
        in_specs=[
            pl.BlockSpec((d, 1), lambda l, j: (0, 0)),
            pl.BlockSpec((None, d, tn), lambda l, j: (l, 0, j)),
            pl.BlockSpec((None, 1, tn), lambda l, j: (l, 0, j)),
        ],
        out_specs=pl.BlockSpec((None, 1, tn), lambda l, j: (l, 0, j)),
        compiler_params=_params("parallel", "parallel"),
        name="adaln_mod",
    )(c.reshape(d, 1), w_ada, b_ada.reshape(depth, 1, n))
    return out.reshape(depth, N_MOD, d)


def _mod_rows(mod_ref, sub):
    return tuple(mod_ref[3 * sub + i:3 * sub + i + 1, :] for i in range(3))


def _ln_rows(ln_ref, sub):
    return ln_ref[2 * sub:2 * sub + 1, :], ln_ref[2 * sub + 1:2 * sub + 2, :]


def _ffn_postnorm(x, sub, mod_ref, ln_ref, wg_ref, wu_ref, wd_ref, alpha):
    shift, scale, gate = _mod_rows(mod_ref, sub)
    d_ff = wg_ref.shape[1]
    fc = d_ff // FFN_CHUNKS
    h = (x * (1.0 + scale) + shift).astype(BF16)
    y = jnp.zeros(x.shape, F32)
    for ci in range(FFN_CHUNKS):
        cols = slice(ci * fc, (ci + 1) * fc)
        act = _silu(_dot(h, wg_ref[:, cols])) * _dot(h, wu_ref[:, cols])
        y = y + _dot(act.astype(BF16), wd_ref[cols, :])
    return _post_norm(x, y, gate, FFN_RES, *_ln_rows(ln_ref, sub), alpha)


def _ffn_kernel(x_ref, mod_ref, ln_ref, wg_ref, wu_ref, wd_ref, o_ref, *, alpha):
    o_ref[...] = _ffn_postnorm(x_ref[...], 0, mod_ref, ln_ref, wg_ref, wu_ref, wd_ref, alpha)


def _mixout_ffn_kernel(x_ref, yr_ref, yd_ref, ys_ref, mod_ref, ln_ref, wo_ref, wg_ref, wu_ref, wd_ref, o_ref,
                       *, alpha):
    vw, g = RET_VW, GROUP_W
    y = (_dot(yr_ref[...], wo_ref[0:vw, :]) + _dot(yd_ref[...], wo_ref[vw:vw + g, :])
         + _dot(ys_ref[...], wo_ref[vw + g:vw + 2 * g, :]))
    x1 = _post_norm(x_ref[...], y, _mod_rows(mod_ref, 1)[2], 1.0, *_ln_rows(ln_ref, 1), alpha)
    o_ref[...] = _ffn_postnorm(x1, 2, mod_ref, ln_ref, wg_ref, wu_ref, wd_ref, alpha)


def _ffn_specs(d, d_ff, layer):
    assert d_ff % (FFN_CHUNKS * V7X_LANES) == 0
    return [_layer_block((d, d_ff), layer), _layer_block((d, d_ff), layer), _layer_block((d_ff, d), layer)]


def _rows(tile, width):
    return pl.BlockSpec((tile, width), lambda i: (i, 0))


def _ffn_sublayer(x, layer, mod, ln, wg, wu, wd, alpha):
    s, d = x.shape
    assert s % FFN_ROW_TILE == 0
    return pl.pallas_call(
        functools.partial(_ffn_kernel, alpha=alpha),
        out_shape=jax.ShapeDtypeStruct((s, d), F32),
        grid=(s // FFN_ROW_TILE,),
        in_specs=[_rows(FFN_ROW_TILE, d), _layer_block((N_MOD, d), layer), _layer_block((6, d), layer)]
        + _ffn_specs(d, wg.shape[2], layer),
        out_specs=_rows(FFN_ROW_TILE, d),
        compiler_params=_params("parallel"),
        name="ffn_postnorm",
    )(x, mod, ln, wg, wu, wd)


def _mixout_ffn_sublayer(x, y_ret, y_dil, y_sb, layer, mod, ln, w_out, wg, wu, wd, alpha):
    s, d = x.shape
    assert s % FFN_ROW_TILE == 0
    t = FFN_ROW_TILE
    return pl.pallas_call(
        functools.partial(_mixout_ffn_kernel, alpha=alpha),
        out_shape=jax.ShapeDtypeStruct((s, d), F32),
        grid=(s // t,),
        in_specs=[_rows(t, d), _rows(t, RET_VW), _rows(t, GROUP_W), _rows(t, GROUP_W),
                  _layer_block((N_MOD, d), layer), _layer_block((6, d), layer), _layer_block(w_out.shape[1:], layer)]
        + _ffn_specs(d, wg.shape[2], layer),
        out_specs=_rows(t, d),
        compiler_params=_params("parallel"),
        name="mixout_ffn_postnorm",
    )(x, y_ret, y_dil, y_sb, mod, ln, w_out, wg, wu, wd)


def _rope(v, cos, sin):
    lane = lax.broadcasted_iota(jnp.int32, cos.shape, 1)
    first_half = (lane & (HEAD_DIM - 1)) < (HEAD_DIM // 2)
    halves = []
    for hf in range(GROUP_W // V7X_LANES):
        vh = v[:, hf * V7X_LANES:(hf + 1) * V7X_LANES]
        partner = jnp.where(first_half,
                            pltpu.roll(vh, V7X_LANES - HEAD_DIM // 2, 1),
                            pltpu.roll(vh, HEAD_DIM // 2, 1))
        halves.append(vh * cos + partner * sin)
    return jnp.concatenate(halves, axis=1)


def _tile_rope(within_ref, start_ref):
    cos_r, sin_r = within_ref[0], within_ref[1]
    c0, s0, c0s, s0s = (start_ref[i:i + 1, :] for i in range(4))
    return c0 * cos_r - s0 * sin_r, s0s * cos_r + c0s * sin_r


def _inproj_kernel(x_ref, mod_ref, w_ref, rw_ref, rs_ref, dw_ref, ds_ref,
                   rq_ref, rk_ref, rv_ref, rg_ref, dq_ref, dk_ref, dv_ref, sq_ref, sk_ref, sv_ref):
    shift, scale, _ = _mod_rows(mod_ref, 1)
    h = (x_ref[...] * (1.0 + scale) + shift).astype(BF16)
    qk_scale = HEAD_DIM ** -0.5
    g, vw = GROUP_W, RET_VW

    full = _dot(h, w_ref[...])

    def proj(start, width):
        return full[:, start:start + width]

    cr, sr = _tile_rope(rw_ref, rs_ref)
    cd, sd = _tile_rope(dw_ref, ds_ref)
    rq_ref[...] = _rope(proj(0, g), cr, sr).astype(BF16)
    rk_ref[...] = (_rope(proj(g, g), cr, sr) * qk_scale).astype(BF16)
    rv_ref[...] = proj(2 * g, vw).astype(BF16)
    rg_ref[...] = proj(2 * g + vw, vw)
    base = 2 * g + 2 * vw
    for ref, val in ((dq_ref, _rope(proj(base, g), cd, sd) * qk_scale),
                     (dk_ref, _rope(proj(base + g, g), cd, sd)), (dv_ref, proj(base + 2 * g, g))):
        for sl in range(N_SLABS):
            ref[sl] = val[:, sl * V7X_LANES:(sl + 1) * V7X_LANES]
    sq_ref[...] = (proj(base + 3 * g, g) * qk_scale).astype(BF16)
    sk_ref[...] = proj(base + 4 * g, g).astype(BF16)
    sv_ref[...] = proj(base + 5 * g, g).astype(BF16)


def _in_projection(x, layer, mod, w_in, tables):
    s, d = x.shape
    n = w_in.shape[2]
    g, vw = GROUP_W, RET_VW
    assert n == 8 * g + 2 * vw

    def rows(width):
        return _rows(ROW_TILE, width)

    def flat(width, dtype):
        return jax.ShapeDtypeStruct((s, width), dtype), rows(width)

    slabs = (jax.ShapeDtypeStruct((N_SLABS, s, V7X_LANES), F32),
             pl.BlockSpec((N_SLABS, ROW_TILE, V7X_LANES), lambda i: (0, i, 0)))
    outs = (flat(g, BF16), flat(g, BF16), flat(vw, BF16), flat(vw, F32), slabs, slabs, slabs,
            flat(g, BF16), flat(g, BF16), flat(g, BF16))
    return pl.pallas_call(
        _inproj_kernel,
        out_shape=[o[0] for o in outs],
        grid=(s // ROW_TILE,),
        in_specs=[rows(d), _layer_block((N_MOD, d), layer), _layer_block((d, n), layer)]
        + [_resident((2, ROW_TILE, V7X_LANES)), pl.BlockSpec((None, 4, V7X_LANES), lambda i: (i, 0, 0))] * 2,
        out_specs=[o[1] for o in outs],
        compiler_params=_params("parallel"),
        name="mixer_inproj",
    )(x, mod, w_in, *tables)


def _rope_tables(inv_freq, s):
    half = HEAD_DIM // 2
    lane_freq = jnp.tile(inv_freq, V7X_LANES // half)[None, :]
    sign = jnp.tile(jnp.concatenate([-jnp.ones(half, F32), jnp.ones(half, F32)]), V7X_LANES // HEAD_DIM)[None, :]
    within = jnp.arange(ROW_TILE, dtype=F32)[:, None] * lane_freq
    start = (jnp.arange(s // ROW_TILE, dtype=F32) * ROW_TILE)[:, None] * lane_freq
    c0, s0 = jnp.cos(start), jnp.sin(start)
    return (jnp.stack([jnp.cos(within), jnp.sin(within)]), jnp.stack([c0, s0, sign * c0, sign * s0], axis=1))


def _head_of(idx, width):
    assert width & (width - 1) == 0
    return lax.shift_right_logical(idx, width.bit_length() - 1)


def _mod_pow2(idx, width):
    assert width & (width - 1) == 0
    return idx & (width - 1)


def _log_gamma_of(head):
    lg = jnp.full(head.shape, LOG_GAMMA[N_HEADS - 1], F32)
    for hh in range(N_HEADS - 2, -1, -1):
        lg = jnp.where(head == hh, LOG_GAMMA[hh], lg)
    return lg


def _own_head_lanes(rows):
    r = lax.broadcasted_iota(jnp.int32, (N_HEADS * rows, GROUP_W), 0)
    l = lax.broadcasted_iota(jnp.int32, (N_HEADS * rows, GROUP_W), 1)
    return _head_of(r, rows) == _head_of(l, HEAD_DIM)


def _stack_heads(q, rows):
    qf = q.astype(F32)
    return jnp.where(_own_head_lanes(rows), jnp.concatenate([qf] * N_HEADS, axis=0), 0.0).astype(q.dtype)


def _merge_heads(stacked, rows):
    l = lax.broadcasted_iota(jnp.int32, (rows, GROUP_W), 1)
    head = _head_of(l, HEAD_DIM)
    out = jnp.zeros((rows, GROUP_W), stacked.dtype)
    for hh in range(N_HEADS):
        out = jnp.where(head == hh, stacked[hh * rows:(hh + 1) * rows, :], out)
    return out


def _ret_kernel(q_ref, k_ref, v_ref, g_ref, o_ref, state_ref, decay_ref, xi_ref, zeta_ref, gc_ref):
    c = RET_BLOCK
    rows = N_HEADS * c

    @pl.when(pl.program_id(0) == 0)
    def _init():
        state_ref[...] = jnp.zeros(state_ref.shape, F32)
        r = lax.broadcasted_iota(jnp.int32, (rows, c), 0)
        j = lax.broadcasted_iota(jnp.int32, (rows, c), 1)
        diff = (_mod_pow2(r, c) - j).astype(F32)
        lg = _log_gamma_of(_head_of(r, c))
        decay_ref[...] = jnp.where(diff >= 0, jnp.exp(lg * jnp.maximum(diff, 0.0)), 0.0)
        r = lax.broadcasted_iota(jnp.int32, (rows, RET_DV), 0)
        xi_ref[...] = jnp.exp(_log_gamma_of(_head_of(r, c)) * (_mod_pow2(r, c).astype(F32) + 1.0))
        i = lax.broadcasted_iota(jnp.int32, (c, GROUP_W), 0)
        l = lax.broadcasted_iota(jnp.int32, (c, GROUP_W), 1)
        zeta_ref[...] = jnp.exp(_log_gamma_of(_head_of(l, HEAD_DIM)) * (c - 1.0 - i.astype(F32)))
        r = lax.broadcasted_iota(jnp.int32, (GROUP_W, RET_DV), 0)
        gc_ref[...] = jnp.exp(_log_gamma_of(_head_of(r, HEAD_DIM)) * float(c))

    state = state_ref[...]
    gc = gc_ref[...]
    for ci in range(RET_CHUNKS_PER_STEP):
        rsl = slice(ci * c, (ci + 1) * c)
        q, k, v = q_ref[rsl, :], k_ref[rsl, :], v_ref[rsl, :]
        qs = _stack_heads(q, c)
        scores = (_dot_nt(qs, k) * decay_ref[...]).astype(BF16)
        inner = jnp.concatenate(
            [_dot(scores[hh * c:(hh + 1) * c, :], v[:, hh * RET_DV:(hh + 1) * RET_DV]) for hh in range(N_HEADS)],
            axis=0)
        o = inner + _dot(qs, state.astype(BF16)) * xi_ref[...]
        mu = jnp.mean(o, axis=-1, keepdims=True)
        oc = o - mu
        var = jnp.mean(oc * oc, axis=-1, keepdims=True)
        on = oc * lax.rsqrt(var + GN_EPS)
        for hh in range(N_HEADS):
            lanes = slice(hh * RET_DV, (hh + 1) * RET_DV)
            o_ref[rsl, lanes] = (_silu(g_ref[rsl, lanes]) * on[hh * c:(hh + 1) * c, :]).astype(o_ref.dtype)
        kz_t = (k.astype(F32) * zeta_ref[...]).T.astype(BF16)
        kv = _dot(kz_t, v)
        state = jnp.concatenate(
            [state[hh * HEAD_DIM:(hh + 1) * HEAD_DIM, :] * gc[hh * HEAD_DIM:(hh + 1) * HEAD_DIM, :]
             + kv[hh * HEAD_DIM:(hh + 1) * HEAD_DIM, hh * RET_DV:(hh + 1) * RET_DV] for hh in range(N_HEADS)], axis=0)
    state_ref[...] = state


def _retention(rq, rk, rv, rg):
    s = rq.shape[0]
    c = RET_BLOCK
    step_rows = c * RET_CHUNKS_PER_STEP
    assert s % step_rows == 0

    def rows(width):
        return _rows(step_rows, width)

    return pl.pallas_call(
        _ret_kernel,
        out_shape=jax.ShapeDtypeStruct((s, RET_VW), BF16),
        grid=(s // step_rows,),
        in_specs=[rows(GROUP_W), rows(GROUP_W), rows(RET_VW), rows(RET_VW)],
        out_specs=rows(RET_VW),
        scratch_shapes=[pltpu.VMEM((GROUP_W, RET_DV), F32), pltpu.VMEM((N_HEADS * c, c), F32),
                        pltpu.VMEM((N_HEADS * c, RET_DV), F32), pltpu.VMEM((c, GROUP_W), F32),
                        pltpu.VMEM((GROUP_W, RET_DV), F32)],
        compiler_params=_params("arbitrary"),
        name="retention",
    )(rq, rk, rv, rg)


def _load_rows(ref, start, rows, stride):
    idx = pl.ds(start, rows) if stride == 1 else pl.ds(start, rows, stride=stride)
    return jnp.concatenate([ref[sl, idx, :] for sl in range(N_SLABS)], axis=1)


def _store_rows(ref, start, rows, stride, val):
    idx = pl.ds(start, rows) if stride == 1 else pl.ds(start, rows, stride=stride)
    for sl in range(N_SLABS):
        ref[sl, idx, :] = val[:, sl * V7X_LANES:(sl + 1) * V7X_LANES]


def _dil_kernel(q_ref, k_ref, v_ref, o_ref, kwin_ref, vwin_ref, part_o_ref, part_l_ref, const_ref):
    t, blk = DIL_TILE, DIL_BLOCK
    b = pl.program_id(0)

    @pl.when(b == 0)
    def _first():
        kwin_ref[:, 0:blk, :] = jnp.zeros((N_SLABS, blk, V7X_LANES), F32)
        vwin_ref[:, 0:blk, :] = jnp.zeros((N_SLABS, blk, V7X_LANES), F32)

    @pl.when(b > 0)
    def _shift():
        kwin_ref[:, 0:blk, :] = kwin_ref[:, blk:2 * blk, :]
        vwin_ref[:, 0:blk, :] = vwin_ref[:, blk:2 * blk, :]

    kwin_ref[:, blk:2 * blk, :] = k_ref[...]
    vwin_ref[:, blk:2 * blk, :] = v_ref[...]

    rows = N_HEADS * t
    qi = _mod_pow2(lax.broadcasted_iota(jnp.int32, (rows, 2 * t), 0), t)
    kc = lax.broadcasted_iota(jnp.int32, (rows, 2 * t), 1)
    ahead = kc - qi
    band = jnp.where(ahead >= 0, jnp.where(ahead <= t, 0.0, MASKED), MASKED)
    const_ref[0] = jnp.where(_own_head_lanes(t), 1.0, 0.0)
    const_ref[1] = band
    const_ref[2] = jnp.where(kc >= t, band, MASKED)

    for pi, dil in enumerate(DILATIONS):

        def unit(u, carry, pi=pi, dil=dil):
            res = _mod_pow2(u, dil)
            ct = lax.shift_right_logical(u, dil.bit_length() - 1)
            q_start = res + ct * (dil * t)
            k_start = blk + q_start - dil * t
            mask_id = jnp.where(jnp.logical_and(b == 0, ct == 0), 2, 1)
            qf = _load_rows(q_ref, q_start, t, dil)
            qs = (jnp.concatenate([qf] * N_HEADS, axis=0) * const_ref[0]).astype(BF16)
            kk = _load_rows(kwin_ref, k_start, 2 * t, dil).astype(BF16)
            vv = _load_rows(vwin_ref, k_start, 2 * t, dil).astype(BF16)
            z = _dot_nt(qs, kk) + const_ref[mask_id]
            m = jnp.max(z, axis=-1, keepdims=True)
            p = jnp.exp(z - m)
            l = jnp.sum(p, axis=-1, keepdims=True)
            o = _dot(p.astype(BF16), vv) / l
            lse = jnp.broadcast_to(m + jnp.log(l), (rows, GROUP_W))
            _store_rows(part_o_ref.at[pi], q_start, t, dil, _merge_heads(o, t))
            _store_rows(part_l_ref.at[pi], q_start, t, dil, _merge_heads(lse, t))
            return carry

        lax.fori_loop(0, blk // t, unit, 0, unroll=8)

    def combine(ci, carry):
        r0 = pl.multiple_of(ci * t, t)
        for sl in range(N_SLABS):
            ls = [part_l_ref[pi, sl, pl.ds(r0, t), :] for pi in range(len(DILATIONS))]
            top = functools.reduce(jnp.maximum, ls)
            ws = [jnp.exp(x - top) for x in ls]
            num = sum(w * part_o_ref[pi, sl, pl.ds(r0, t), :] for pi, w in enumerate(ws))
            o_ref[pl.ds(r0, t), sl * V7X_LANES:(sl + 1) * V7X_LANES] = (num / sum(ws)).astype(o_ref.dtype)
        return carry

    lax.fori_loop(0, blk // t, combine, 0)


def _dilated_attention(dq, dk, dv):
    s = dq.shape[1]
    blk = DIL_BLOCK
    assert s % blk == 0 and all(w == DIL_TILE * d for w, d in zip(WINDOWS, DILATIONS))
    assert 2 * DIL_TILE == GROUP_W
    slab = pl.BlockSpec((N_SLABS, blk, V7X_LANES), lambda i: (0, i, 0))
    n_pat = len(DILATIONS)
    return pl.pallas_call(
        _dil_kernel,
        out_shape=jax.ShapeDtypeStruct((s, GROUP_W), BF16),
        grid=(s // blk,),
        in_specs=[slab, slab, slab],
        out_specs=pl.BlockSpec((blk, GROUP_W), lambda i: (i, 0)),
        scratch_shapes=[pltpu.VMEM((N_SLABS, 2 * blk, V7X_LANES), F32), pltpu.VMEM((N_SLABS, 2 * blk, V7X_LANES), F32),
                        pltpu.VMEM((n_pat, N_SLABS, blk, V7X_LANES), F32),
                        pltpu.VMEM((n_pat, N_SLABS, blk, V7X_LANES), F32),
                        pltpu.VMEM((3, N_HEADS * DIL_TILE, 2 * DIL_TILE), F32)],
        compiler_params=_params("arbitrary"),
        name="dilated_attn",
    )(dq, dk, dv)


def _sb_kernel(q_ref, k_ref, v_ref, o_ref, acc_ref, carry_ref):
    t = SB_TILE
    rows = N_HEADS * t
    r = lax.broadcasted_iota(jnp.int32, (rows, t), 0)
    col = lax.broadcasted_iota(jnp.int32, (rows, t), 1)
    causal = _mod_pow2(r, t) > col
    kj = lax.broadcasted_iota(jnp.int32, (t, t), 0)
    ks = lax.broadcasted_iota(jnp.int32, (t, t), 1)
    later = jnp.where(kj > ks, 1.0, 0.0).astype(BF16)

    def visit(qi, qs, jj, carry, on_diagonal=False, live=None):
        k0 = pl.multiple_of(jnp.maximum(qi - jj, 0) * t, t)
        z = _dot_nt(qs, k_ref[pl.ds(k0, t), :])
        softplus = jnp.maximum(z, 0.0) + jnp.log(1.0 + jnp.exp(-jnp.abs(z)))
        log_keep = -softplus
        if on_diagonal:
            log_keep = jnp.where(causal, log_keep, 0.0)
        if live is not None:
            log_keep = log_keep * live
        within = _dot(log_keep.astype(BF16), later)
        a = jnp.exp((z - softplus) + within + carry)
        if on_diagonal:
            a = jnp.where(causal, a, 0.0)
        if live is not None:
            a = a * live
        return _dot(a.astype(BF16), v_ref[pl.ds(k0, t), :]), carry + jnp.sum(log_keep, axis=-1, keepdims=True)

    pending = []
    for bi in range(SB_TILES_PER_STEP):
        qi = pl.program_id(0) * SB_TILES_PER_STEP + bi
        qs = _stack_heads(q_ref[bi * t:(bi + 1) * t, :], t)
        out0, carry0 = visit(qi, qs, 0, jnp.zeros((rows, 1), F32), on_diagonal=True)
        out1, carry1 = visit(qi, qs, 1, carry0, live=jnp.where(qi >= 1, 1.0, 0.0))
        acc_ref[bi] = out0 + out1
        carry_ref[bi] = carry1
        pending.append((qi, qs, jnp.max(carry1)))

    for bi, (qi, qs, carry_max) in enumerate(pending):

        def unfinished(state, qi=qi):
            jj, carry_max = state
            return jnp.logical_and(jj <= qi, carry_max > SB_STOP)

        def sweep(state, bi=bi, qi=qi, qs=qs):
            jj, _ = state
            out, carry = visit(qi, qs, jj, carry_ref[bi])
            acc_ref[bi] += out
            carry_ref[bi] = carry
            return jj + 1, jnp.max(carry)

        lax.while_loop(unfinished, sweep, (jnp.int32(2), carry_max))
        o_ref[bi * t:(bi + 1) * t, :] = _merge_heads(acc_ref[bi], t).astype(o_ref.dtype)


def _stick_breaking(sq, sk, sv):
    s = sq.shape[0]
    t = SB_TILE
    step_rows = t * SB_TILES_PER_STEP
    assert s % step_rows == 0
    rows = _rows(step_rows, GROUP_W)
    return pl.pallas_call(
        _sb_kernel,
        out_shape=jax.ShapeDtypeStruct((s, GROUP_W), BF16),
        grid=(s // step_rows,),
        in_specs=[rows, _resident((s, GROUP_W)), _resident((s, GROUP_W))],
        out_specs=rows,
        scratch_shapes=[pltpu.VMEM((SB_TILES_PER_STEP, N_HEADS * t, GROUP_W), F32),
                        pltpu.VMEM((SB_TILES_PER_STEP, N_HEADS * t, 1), F32)],
        compiler_params=_params("parallel"),
        name="stickbreak_attn",
    )(sq, sk, sv)


def kernel(x, c, w_ada, b_ada, ln_gain, ln_bias, ffn1_w_gate, ffn1_w_up, ffn1_w_down,
           w_in, w_out, ffn2_w_gate, ffn2_w_up, ffn2_w_down):
    batch, s, d = x.shape
    depth = w_ada.shape[0]
    assert batch == 1 and c.shape == (1, d)
    alpha = (2.0 * depth) ** 0.25
    half = HEAD_DIM // 2
    ret_freq = 1.0 / (ROPE_THETA ** jnp.linspace(0.0, 1.0, half, dtype=F32))
    rope_freq = 1.0 / (ROPE_THETA ** (jnp.arange(0, HEAD_DIM, 2, dtype=F32) / HEAD_DIM))
    tables = _rope_tables(ret_freq, s) + _rope_tables(rope_freq, s)
    mod = _modulation(c, w_ada, b_ada)
    ln = jnp.stack([ln_gain, ln_bias], axis=2).reshape(depth, 6, d)
    f1g, f1u, f1d, win, wout, f2g, f2u, f2d = (
        w.astype(BF16) for w in (ffn1_w_gate, ffn1_w_up, ffn1_w_down, w_in, w_out, ffn2_w_gate, ffn2_w_up, ffn2_w_down))
    xs = x[0]
    for l in range(depth):
        xs = _ffn_sublayer(xs, l, mod, ln, f1g, f1u, f1d, alpha)
        rq, rk, rv, rg, dq, dk, dv, sq, sk, sv = _in_projection(xs, l, mod, win, tables)
        y_ret = _retention(rq, rk, rv, rg)
        y_dil = _dilated_attention(dq, dk, dv)
        y_sb = _stick_breaking(sq, sk, sv)
        xs = _mixout_ffn_sublayer(xs, y_ret, y_dil, y_sb, l, mod, ln, wout, f2g, f2u, f2d, alpha)
    return xs[None]
```

```python
import functools
import math

import jax
import jax.numpy as jnp
from jax import lax
from jax.experimental import pallas as pl
from jax.experimental.pallas import tpu as pltpu

HEAD_DIM = 64
N_HEADS = 4
RET_DV = 128
GROUP_W = N_HEADS * HEAD_DIM
RET_VW = N_HEADS * RET_DV
WINDOWS = (128, 512, 2048)
DILATIONS = (1, 4, 16)
ROPE_THETA = 10000.0
LN_EPS = 1e-5
GN_EPS = 1e-6
FFN_RES = 0.5
N_MOD = 9
LOG_GAMMA = tuple(math.log1p(-(2.0 ** (-5 - h))) for h in range(N_HEADS))

V7X_LANES = 128
N_SLABS = GROUP_W // V7X_LANES
V7X_VMEM_LIMIT_BYTES = 56 * 1024 * 1024
MOD_COLS = 2304
ROW_TILE = 512
FFN_ROW_TILE = 1024
FFN_CHUNKS = 11
RET_BLOCK = 256
RET_CHUNKS_PER_STEP = 8
DIL_TILE = 128
DIL_BLOCK = DIL_TILE * max(DILATIONS)
SB_TILE = 256
SB_TILES_PER_STEP = 4
MASKED = -1e30
SB_STOP = -110.0

F32 = jnp.float32
BF16 = jnp.bfloat16


def _params(*sem):
    return pltpu.CompilerParams(dimension_semantics=sem, vmem_limit_bytes=V7X_VMEM_LIMIT_BYTES)


def _resident(shape):
    return pl.BlockSpec(shape, lambda *_: (0,) * len(shape), pipeline_mode=pl.Buffered(1))


def _layer_block(tail, layer):
    return pl.BlockSpec((None,) + tuple(tail), lambda *_: (layer,) + (0,) * len(tail), pipeline_mode=pl.Buffered(1))


def _dot(a, b):
    return jnp.dot(a, b, preferred_element_type=F32)


def _dot_nt(a, b):
    return lax.dot_general(a, b, (((1,), (1,)), ((), ())), preferred_element_type=F32)


def _silu(v):
    return v * jax.nn.sigmoid(v)


def _post_norm(x, y, gate, res_w, gain, bias, alpha):
    z = alpha * x + (res_w * (1.0 + gate)) * y
    mu = jnp.mean(z, axis=-1, keepdims=True)
    zc = z - mu
    var = jnp.mean(zc * zc, axis=-1, keepdims=True)
    return zc * lax.rsqrt(var + LN_EPS) * gain + bias


def _mod_kernel(c_ref, w_ref, b_ref, o_ref):
    c = c_ref[...]
    o_ref[...] = jnp.sum(_silu(c) * w_ref[...], axis=0, keepdims=True) + b_ref[...]


def _modulation(c, w_ada, b_ada):
    depth, d, n = w_ada.shape
    tn = MOD_COLS
    assert n % tn == 0
    out = pl.pallas_call(
        _mod_kernel,
        out_shape=jax.ShapeDtypeStruct((depth, 1, n), F32),
        grid=(depth, n // tn),
        in_specs=[
            pl.BlockSpec((d, 1), lambda l, j: (0, 0)),
            pl.BlockSpec((None, d, tn), lambda l, j: (l, 0, j)),
            pl.BlockSpec((None, 1, tn), lambda l, j: (l, 0, j)),
        ],
        out_specs=pl.BlockSpec((None, 1, tn), lambda l, j: (l, 0, j)),
        compiler_params=_params("parallel", "parallel"),
        name="adaln_mod",
    )(c.reshape(d, 1), w_ada, b_ada.reshape(depth, 1, n))
    return out.reshape(depth, N_MOD, d)


def _mod_rows(mod_ref, sub):
    return tuple(mod_ref[3 * sub + i:3 * sub + i + 1, :] for i in range(3))


def _ln_rows(ln_ref, sub):
    return ln_ref[2 * sub:2 * sub + 1, :], ln_ref[2 * sub + 1:2 * sub + 2, :]


def _ffn_postnorm(x, sub, mod_ref, ln_ref, wg_ref, wu_ref, wd_ref, alpha):
    shift, scale, gate = _mod_rows(mod_ref, sub)
    d_ff = wg_ref.shape[1]
    fc = d_ff // FFN_CHUNKS
    h = (x * (1.0 + scale) + shift).astype(BF16)
    y = jnp.zeros(x.shape, F32)
    for ci in range(FFN_CHUNKS):
        cols = slice(ci * fc, (ci + 1) * fc)
        act = _silu(_dot(h, wg_ref[:, cols])) * _dot(h, wu_ref[:, cols])
        y = y + _dot(act.astype(BF16), wd_ref[cols, :])
    return _post_norm(x, y, gate, FFN_RES, *_ln_rows(ln_ref, sub), alpha)


def _ffn_kernel(x_ref, mod_ref, ln_ref, wg_ref, wu_ref, wd_ref, o_ref, *, alpha):
    o_ref[...] = _ffn_postnorm(x_ref[...], 0, mod_ref, ln_ref, wg_ref, wu_ref, wd_ref, alpha)


def _mixout_ffn_kernel(x_ref, yr_ref, yd_ref, ys_ref, mod_ref, ln_ref, wo_ref, wg_ref, wu_ref, wd_ref, o_ref,
                       *, alpha):
    vw, g = RET_VW, GROUP_W
    y = (_dot(yr_ref[...], wo_ref[0:vw, :]) + _dot(yd_ref[...], wo_ref[vw:vw + g, :])
         + _dot(ys_ref[...], wo_ref[vw + g:vw + 2 * g, :]))
    x1 = _post_norm(x_ref[...], y, _mod_rows(mod_ref, 1)[2], 1.0, *_ln_rows(ln_ref, 1), alpha)
    o_ref[...] = _ffn_postnorm(x1, 2, mod_ref, ln_ref, wg_ref, wu_ref, wd_ref, alpha)


def _ffn_specs(d, d_ff, layer):
    assert d_ff % (FFN_CHUNKS * V7X_LANES) == 0
    return [_layer_block((d, d_ff), layer), _layer_block((d, d_ff), layer), _layer_block((d_ff, d), layer)]


def _rows(tile, width):
    return pl.BlockSpec((tile, width), lambda i: (i, 0))


def _ffn_sublayer(x, layer, mod, ln, wg, wu, wd, alpha):
    s, d = x.shape
    assert s % FFN_ROW_TILE == 0
    return pl.pallas_call(
        functools.partial(_ffn_kernel, alpha=alpha),
        out_shape=jax.ShapeDtypeStruct((s, d), F32),
        grid=(s // FFN_ROW_TILE,),
        in_specs=[_rows(FFN_ROW_TILE, d), _layer_block((N_MOD, d), layer), _layer_block((6, d), layer)]
        + _ffn_specs(d, wg.shape[2], layer),
        out_specs=_rows(FFN_ROW_TILE, d),
        compiler_params=_params("parallel"),
        name="ffn_postnorm",
    )(x, mod, ln, wg, wu, wd)


def _mixout_ffn_sublayer(x, y_ret, y_dil, y_sb, layer, mod, ln, w_out, wg, wu, wd, alpha):
    s, d = x.shape
    assert s % FFN_ROW_TILE == 0
    t = FFN_ROW_TILE
    return pl.pallas_call(
        functools.partial(_mixout_ffn_kernel, alpha=alpha),
        out_shape=jax.ShapeDtypeStruct((s, d), F32),
        grid=(s // t,),
        in_specs=[_rows(t, d), _rows(t, RET_VW), _rows(t, GROUP_W), _rows(t, GROUP_W),
                  _layer_block((N_MOD, d), layer), _layer_block((6, d), layer), _layer_block(w_out.shape[1:], layer)]
        + _ffn_specs(d, wg.shape[2], layer),
        out_specs=_rows(t, d),
        compiler_params=_params("parallel"),
        name="mixout_ffn_postnorm",
    )(x, y_ret, y_dil, y_sb, mod, ln, w_out, wg, wu, wd)


def _rope(v, cos, sin):
    lane = lax.broadcasted_iota(jnp.int32, cos.shape, 1)
    first_half = (lane & (HEAD_DIM - 1)) < (HEAD_DIM // 2)
    halves = []
    for hf in range(GROUP_W // V7X_LANES):
        vh = v[:, hf * V7X_LANES:(hf + 1) * V7X_LANES]
        partner = jnp.where(first_half,
                            pltpu.roll(vh, V7X_LANES - HEAD_DIM // 2, 1),
                            pltpu.roll(vh, HEAD_DIM // 2, 1))
        halves.append(vh * cos + partner * sin)
    return jnp.concatenate(halves, axis=1)


def _tile_rope(within_ref, start_ref):
    cos_r, sin_r = within_ref[0], within_ref[1]
    c0, s0, c0s, s0s = (start_ref[i:i + 1, :] for i in range(4))
    return c0 * cos_r - s0 * sin_r, s0s * cos_r + c0s * sin_r


def _inproj_kernel(x_ref, mod_ref, w_ref, rw_ref, rs_ref, dw_ref, ds_ref,
                   rq_ref, rk_ref, rv_ref, rg_ref, dq_ref, dk_ref, dv_ref, sq_ref, sk_ref, sv_ref):
    shift, scale, _ = _mod_rows(mod_ref, 1)
    h = (x_ref[...] * (1.0 + scale) + shift).astype(BF16)
    qk_scale = HEAD_DIM ** -0.5
    g, vw = GROUP_W, RET_VW

    full = _dot(h, w_ref[...])

    def proj(start, width):
        return full[:, start:start + width]

    cr, sr = _tile_rope(rw_ref, rs_ref)
    cd, sd = _tile_rope(dw_ref, ds_ref)
    rq_ref[...] = _rope(proj(0, g), cr, sr).astype(BF16)
    rk_ref[...] = (_rope(proj(g, g), cr, sr) * qk_scale).astype(BF16)
    rv_ref[...] = proj(2 * g, vw).astype(BF16)
    rg_ref[...] = proj(2 * g + vw, vw)
    base = 2 * g + 2 * vw
    for ref, val in ((dq_ref, _rope(proj(base, g), cd, sd) * qk_scale),
                     (dk_ref, _rope(proj(base + g, g), cd, sd)), (dv_ref, proj(base + 2 * g, g))):
        for sl in range(N_SLABS):
            ref[sl] = val[:, sl * V7X_LANES:(sl + 1) * V7X_LANES]
    sq_ref[...] = (proj(base + 3 * g, g) * qk_scale).astype(BF16)
    sk_ref[...] = proj(base + 4 * g, g).astype(BF16)
    sv_ref[...] = proj(base + 5 * g, g).astype(BF16)


def _in_projection(x, layer, mod, w_in, tables):
    s, d = x.shape
    n = w_in.shape[2]
    g, vw = GROUP_W, RET_VW
    assert n == 8 * g + 2 * vw

    def rows(width):
        return _rows(ROW_TILE, width)

    def flat(width, dtype):
        return jax.ShapeDtypeStruct((s, width), dtype), rows(width)

    slabs = (jax.ShapeDtypeStruct((N_SLABS, s, V7X_LANES), F32),
             pl.BlockSpec((N_SLABS, ROW_TILE, V7X_LANES), lambda i: (0, i, 0)))
    outs = (flat(g, BF16), flat(g, BF16), flat(vw, BF16), flat(vw, F32), slabs, slabs, slabs,
            flat(g, BF16), flat(g, BF16), flat(g, BF16))
    return pl.pallas_call(
        _inproj_kernel,
        out_shape=[o[0] for o in outs],
        grid=(s // ROW_TILE,),
        in_specs=[rows(d), _layer_block((N_MOD, d), layer), _layer_block((d, n), layer)]
        + [_resident((2, ROW_TILE, V7X_LANES)), pl.BlockSpec((None, 4, V7X_LANES), lambda i: (i, 0, 0))] * 2,
        out_specs=[o[1] for o in outs],
        compiler_params=_params("parallel"),
        name="mixer_inproj",
    )(x, mod, w_in, *tables)


def _rope_tables(inv_freq, s):
    half = HEAD_DIM // 2
    lane_freq = jnp.tile(inv_freq, V7X_LANES // half)[None, :]
    sign = jnp.tile(jnp.concatenate([-jnp.ones(half, F32), jnp.ones(half, F32)]), V7X_LANES // HEAD_DIM)[None, :]
    within = jnp.arange(ROW_TILE, dtype=F32)[:, None] * lane_freq
    start = (jnp.arange(s // ROW_TILE, dtype=F32) * ROW_TILE)[:, None] * lane_freq
    c0, s0 = jnp.cos(start), jnp.sin(start)
    return (jnp.stack([jnp.cos(within), jnp.sin(within)]), jnp.stack([c0, s0, sign * c0, sign * s0], axis=1))


def _head_of(idx, width):
    assert width & (width - 1) == 0
    return lax.shift_right_logical(idx, width.bit_length() - 1)


def _mod_pow2(idx, width):
    assert width & (width - 1) == 0
    return idx & (width - 1)


def _log_gamma_of(head):
    lg = jnp.full(head.shape, LOG_GAMMA[N_HEADS - 1], F32)
    for hh in range(N_HEADS - 2, -1, -1):
        lg = jnp.where(head == hh, LOG_GAMMA[hh], lg)
    return lg


def _own_head_lanes(rows):
    r = lax.broadcasted_iota(jnp.int32, (N_HEADS * rows, GROUP_W), 0)
    l = lax.broadcasted_iota(jnp.int32, (N_HEADS * rows, GROUP_W), 1)
    return _head_of(r, rows) == _head_of(l, HEAD_DIM)


def _stack_heads(q, rows):
    qf = q.astype(F32)
    return jnp.where(_own_head_lanes(rows), jnp.concatenate([qf] * N_HEADS, axis=0), 0.0).astype(q.dtype)


def _merge_heads(stacked, rows):
    l = lax.broadcasted_iota(jnp.int32, (rows, GROUP_W), 1)
    head = _head_of(l, HEAD_DIM)
    out = jnp.zeros((rows, GROUP_W), stacked.dtype)
    for hh in range(N_HEADS):
        out = jnp.where(head == hh, stacked[hh * rows:(hh + 1) * rows, :], out)
    return out


def _ret_kernel(q_ref, k_ref, v_ref, g_ref, o_ref, state_ref, decay_ref, xi_ref, zeta_ref, gc_ref):
    c = RET_BLOCK
    rows = N_HEADS * c

    @pl.when(pl.program_id(0) == 0)
    def _init():
        state_ref[...] = jnp.zeros(state_ref.shape, F32)
        r = lax.broadcasted_iota(jnp.int32, (rows, c), 0)
        j = lax.broadcasted_iota(jnp.int32, (rows, c), 1)
        diff = (_mod_pow2(r, c) - j).astype(F32)
        lg = _log_gamma_of(_head_of(r, c))
        decay_ref[...] = jnp.where(diff >= 0, jnp.exp(lg * jnp.maximum(diff, 0.0)), 0.0)
        r = lax.broadcasted_iota(jnp.int32, (rows, RET_DV), 0)
        xi_ref[...] = jnp.exp(_log_gamma_of(_head_of(r, c)) * (_mod_pow2(r, c).astype(F32) + 1.0))
        i = lax.broadcasted_iota(jnp.int32, (c, GROUP_W), 0)
        l = lax.broadcasted_iota(jnp.int32, (c, GROUP_W), 1)
        zeta_ref[...] = jnp.exp(_log_gamma_of(_head_of(l, HEAD_DIM)) * (c - 1.0 - i.astype(F32)))
        r = lax.broadcasted_iota(jnp.int32, (GROUP_W, RET_DV), 0)
        gc_ref[...] = jnp.exp(_log_gamma_of(_head_of(r, HEAD_DIM)) * float(c))

    state = state_ref[...]
    gc = gc_ref[...]
    for ci in range(RET_CHUNKS_PER_STEP):
        rsl = slice(ci * c, (ci + 1) * c)
        q, k, v = q_ref[rsl, :], k_ref[rsl, :], v_ref[rsl, :]
        qs = _stack_heads(q, c)
        scores = (_dot_nt(qs, k) * decay_ref[...]).astype(BF16)
        inner = jnp.concatenate(
            [_dot(scores[hh * c:(hh + 1) * c, :], v[:, hh * RET_DV:(hh + 1) * RET_DV]) for hh in range(N_HEADS)],
            axis=0)
        o = inner + _dot(qs, state.astype(BF16)) * xi_ref[...]
        mu = jnp.mean(o, axis=-1, keepdims=True)
        oc = o - mu
        var = jnp.mean(oc * oc, axis=-1, keepdims=True)
        on = oc * lax.rsqrt(var + GN_EPS)
        for hh in range(N_HEADS):
            lanes = slice(hh * RET_DV, (hh + 1) * RET_DV)
            o_ref[rsl, lanes] = (_silu(g_ref[rsl, lanes]) * on[hh * c:(hh + 1) * c, :]).astype(o_ref.dtype)
        kz_t = (k.astype(F32) * zeta_ref[...]).T.astype(BF16)
        kv = _dot(kz_t, v)
        state = jnp.concatenate(
            [state[hh * HEAD_DIM:(hh + 1) * HEAD_DIM, :] * gc[hh * HEAD_DIM:(hh + 1) * HEAD_DIM, :]
             + kv[hh * HEAD_DIM:(hh + 1) * HEAD_DIM, hh * RET_DV:(hh + 1) * RET_DV] for hh in range(N_HEADS)], axis=0)
    state_ref[...] = state


def _retention(rq, rk, rv, rg):
    s = rq.shape[0]
    c = RET_BLOCK
    step_rows = c * RET_CHUNKS_PER_STEP
    assert s % step_rows == 0

    def rows(width):
        return _rows(step_rows, width)

    return pl.pallas_call(
        _ret_kernel,
        out_shape=jax.ShapeDtypeStruct((s, RET_VW), BF16),
        grid=(s // step_rows,),
        in_specs=[rows(GROUP_W), rows(GROUP_W), rows(RET_VW), rows(RET_VW)],
        out_specs=rows(RET_VW),
        scratch_shapes=[pltpu.VMEM((GROUP_W, RET_DV), F32), pltpu.VMEM((N_HEADS * c, c), F32),
                        pltpu.VMEM((N_HEADS * c, RET_DV), F32), pltpu.VMEM((c, GROUP_W), F32),
                        pltpu.VMEM((GROUP_W, RET_DV), F32)],
        compiler_params=_params("arbitrary"),
        name="retention",
    )(rq, rk, rv, rg)


def _load_rows(ref, start, rows, stride):
    idx = pl.ds(start, rows) if stride == 1 else pl.ds(start, rows, stride=stride)
    return jnp.concatenate([ref[sl, idx, :] for sl in range(N_SLABS)], axis=1)


def _store_rows(ref, start, rows, stride, val):
    idx = pl.ds(start, rows) if stride == 1 else pl.ds(start, rows, stride=stride)
    for sl in range(N_SLABS):
        ref[sl, idx, :] = val[:, sl * V7X_LANES:(sl + 1) * V7X_LANES]


def _dil_kernel(q_ref, k_ref, v_ref, o_ref, kwin_ref, vwin_ref, part_o_ref, part_l_ref, const_ref):
    t, blk = DIL_TILE, DIL_BLOCK
    b = pl.program_id(0)

    @pl.when(b == 0)
    def _first():
        kwin_ref[:, 0:blk, :] = jnp.zeros((N_SLABS, blk, V7X_LANES), F32)
        vwin_ref[:, 0:blk, :] = jnp.zeros((N_SLABS, blk, V7X_LANES), F32)

    @pl.when(b > 0)
    def _shift():
        kwin_ref[:, 0:blk, :] = kwin_ref[:, blk:2 * blk, :]
        vwin_ref[:, 0:blk, :] = vwin_ref[:, blk:2 * blk, :]

    kwin_ref[:, blk:2 * blk, :] = k_ref[...]
    vwin_ref[:, blk:2 * blk, :] = v_ref[...]

    rows = N_HEADS * t
    qi = _mod_pow2(lax.broadcasted_iota(jnp.int32, (rows, 2 * t), 0), t)
    kc = lax.broadcasted_iota(jnp.int32, (rows, 2 * t), 1)
    ahead = kc - qi
    band = jnp.where(ahead >= 0, jnp.where(ahead <= t, 0.0, MASKED), MASKED)
    const_ref[0] = jnp.where(_own_head_lanes(t), 1.0, 0.0)
    const_ref[1] = band
    const_ref[2] = jnp.where(kc >= t, band, MASKED)

    for pi, dil in enumerate(DILATIONS):

        def unit(u, carry, pi=pi, dil=dil):
            res = _mod_pow2(u, dil)
            ct = lax.shift_right_logical(u, dil.bit_length() - 1)
            q_start = res + ct * (dil * t)
            k_start = blk + q_start - dil * t
            mask_id = jnp.where(jnp.logical_and(b == 0, ct == 0), 2, 1)
            qf = _load_rows(q_ref, q_start, t, dil)
            qs = (jnp.concatenate([qf] * N_HEADS, axis=0) * const_ref[0]).astype(BF16)
            kk = _load_rows(kwin_ref, k_start, 2 * t, dil).astype(BF16)
            vv = _load_rows(vwin_ref, k_start, 2 * t, dil).astype(BF16)
            z = _dot_nt(qs, kk) + const_ref[mask_id]
            m = jnp.max(z, axis=-1, keepdims=True)
            p = jnp.exp(z - m)
            l = jnp.sum(p, axis=-1, keepdims=True)
            o = _dot(p.astype(BF16), vv) / l
            lse = jnp.broadcast_to(m + jnp.log(l), (rows, GROUP_W))
            _store_rows(part_o_ref.at[pi], q_start, t, dil, _merge_heads(o, t))
            _store_rows(part_l_ref.at[pi], q_start, t, dil, _merge_heads(lse, t))
            return carry

        lax.fori_loop(0, blk // t, unit, 0, unroll=8)

    def combine(ci, carry):
        r0 = pl.multiple_of(ci * t, t)
        for sl in range(N_SLABS):
            ls = [part_l_ref[pi, sl, pl.ds(r0, t), :] for pi in range(len(DILATIONS))]
            top = functools.reduce(jnp.maximum, ls)
            ws = [jnp.exp(x - top) for x in ls]
            num = sum(w * part_o_ref[pi, sl, pl.ds(r0, t), :] for pi, w in enumerate(ws))
            o_ref[pl.ds(r0, t), sl * V7X_LANES:(sl + 1) * V7X_LANES] = (num / sum(ws)).astype(o_ref.dtype)
        return carry

    lax.fori_loop(0, blk // t, combine, 0)


def _dilated_attention(dq, dk, dv):
    s = dq.shape[1]
    blk = DIL_BLOCK
    assert s % blk == 0 and all(w == DIL_TILE * d for w, d in zip(WINDOWS, DILATIONS))
    assert 2 * DIL_TILE == GROUP_W
    slab = pl.BlockSpec((N_SLABS, blk, V7X_LANES), lambda i: (0, i, 0))
    n_pat = len(DILATIONS)
    return pl.pallas_call(
        _dil_kernel,
        out_shape=jax.ShapeDtypeStruct((s, GROUP_W), BF16),
        grid=(s // blk,),
        in_specs=[slab, slab, slab],
        out_specs=pl.BlockSpec((blk, GROUP_W), lambda i: (i, 0)),
        scratch_shapes=[pltpu.VMEM((N_SLABS, 2 * blk, V7X_LANES), F32), pltpu.VMEM((N_SLABS, 2 * blk, V7X_LANES), F32),
                        pltpu.VMEM((n_pat, N_SLABS, blk, V7X_LANES), F32),
                        pltpu.VMEM((n_pat, N_SLABS, blk, V7X_LANES), F32),
                        pltpu.VMEM((3, N_HEADS * DIL_TILE, 2 * DIL_TILE), F32)],
        compiler_params=_params("arbitrary"),
        name="dilated_attn",
    )(dq, dk, dv)


def _sb_kernel(q_ref, k_ref, v_ref, o_ref, acc_ref, carry_ref):
    t = SB_TILE
    rows = N_HEADS * t
    r = lax.broadcasted_iota(jnp.int32, (rows, t), 0)
    col = lax.broadcasted_iota(jnp.int32, (rows, t), 1)
    causal = _mod_pow2(r, t) > col
    kj = lax.broadcasted_iota(jnp.int32, (t, t), 0)
    ks = lax.broadcasted_iota(jnp.int32, (t, t), 1)
    later = jnp.where(kj > ks, 1.0, 0.0).astype(BF16)

    def visit(qi, qs, jj, carry, on_diagonal=False, live=None):
        k0 = pl.multiple_of(jnp.maximum(qi - jj, 0) * t, t)
        z = _dot_nt(qs, k_ref[pl.ds(k0, t), :])
        softplus = jnp.maximum(z, 0.0) + jnp.log(1.0 + jnp.exp(-jnp.abs(z)))
        log_keep = -softplus
        if on_diagonal:
            log_keep = jnp.where(causal, log_keep, 0.0)
        if live is not None:
            log_keep = log_keep * live
        within = _dot(log_keep.astype(BF16), later)
        a = jnp.exp((z - softplus) + within + carry)
        if on_diagonal:
            a = jnp.where(causal, a, 0.0)
        if live is not None:
            a = a * live
        return _dot(a.astype(BF16), v_ref[pl.ds(k0, t), :]), carry + jnp.sum(log_keep, axis=-1, keepdims=True)

    pending = []
    for bi in range(SB_TILES_PER_STEP):
        qi = pl.program_id(0) * SB_TILES_PER_STEP + bi
        qs = _stack_heads(q_ref[bi * t:(bi + 1) * t, :], t)
        out0, carry0 = visit(qi, qs, 0, jnp.zeros((rows, 1), F32), on_diagonal=True)
        out1, carry1 = visit(qi, qs, 1, carry0, live=jnp.where(qi >= 1, 1.0, 0.0))
        acc_ref[bi] = out0 + out1
        carry_ref[bi] = carry1
        pending.append((qi, qs, jnp.max(carry1)))

    for bi, (qi, qs, carry_max) in enumerate(pending):

        def unfinished(state, qi=qi):
            jj, carry_max = state
            return jnp.logical_and(jj <= qi, carry_max > SB_STOP)

        def sweep(state, bi=bi, qi=qi, qs=qs):
            jj, _ = state
            out, carry = visit(qi, qs, jj, carry_ref[bi])
            acc_ref[bi] += out
            carry_ref[bi] = carry
            return jj + 1, jnp.max(carry)

        lax.while_loop(unfinished, sweep, (jnp.int32(2), carry_max))
        o_ref[bi * t:(bi + 1) * t, :] = _merge_heads(acc_ref[bi], t).astype(o_ref.dtype)


def _stick_breaking(sq, sk, sv):
    s = sq.shape[0]
    t = SB_TILE
    step_rows = t * SB_TILES_PER_STEP
    assert s % step_rows == 0
    rows = _rows(step_rows, GROUP_W)
    return pl.pallas_call(
        _sb_kernel,
        out_shape=jax.ShapeDtypeStruct((s, GROUP_W), BF16),
        grid=(s // step_rows,),
        in_specs=[rows, _resident((s, GROUP_W)), _resident((s, GROUP_W))],
        out_specs=rows,
        scratch_shapes=[pltpu.VMEM((SB_TILES_PER_STEP, N_HEADS * t, GROUP_W), F32),
                        pltpu.VMEM((SB_TILES_PER_STEP, N_HEADS * t, 1), F32)],
        compiler_params=_params("parallel"),
        name="stickbreak_attn",
    )(sq, sk, sv)


def kernel(x, c, w_ada, b_ada, ln_gain, ln_bias, ffn1_w_gate, ffn1_w_up, ffn1_w_down,
           w_in, w_out, ffn2_w_gate, ffn2_w_up, ffn2_w_down):
    batch, s, d = x.shape
    depth = w_ada.shape[0]
    assert batch == 1 and c.shape == (1, d)
    alpha = (2.0 * depth) ** 0.25
    half = HEAD_DIM // 2
    ret_freq = 1.0 / (ROPE_THETA ** jnp.linspace(0.0, 1.0, half, dtype=F32))
    rope_freq = 1.0 / (ROPE_THETA ** (jnp.arange(0, HEAD_DIM, 2, dtype=F32) / HEAD_DIM))
    tables = _rope_tables(ret_freq, s) + _rope_tables(rope_freq, s)
    mod = _modulation(c, w_ada, b_ada)
    ln = jnp.stack([ln_gain, ln_bias], axis=2).reshape(depth, 6, d)
    f1g, f1u, f1d, win, wout, f2g, f2u, f2d = (
        w.astype(BF16) for w in (ffn1_w_gate, ffn1_w_up, ffn1_w_down, w_in, w_out, ffn2_w_gate, ffn2_w_up, ffn2_w_down))
    xs = x[0]
    for l in range(depth):
        xs = _ffn_sublayer(xs, l, mod, ln, f1g, f1u, f1d, alpha)
        rq, rk, rv, rg, dq, dk, dv, sq, sk, sv = _in_projection(xs, l, mod, win, tables)
        y_ret = _retention(rq, rk, rv, rg)
        y_dil = _dilated_attention(dq, dk, dv)
        y_sb = _stick_breaking(sq, sk, sv)
        xs = _mixout_ffn_sublayer(xs, y_ret, y_dil, y_sb, l, mod, ln, wout, f2g, f2u, f2d, alpha)
    return xs[None]
```

```python
import functools
import math

import jax
import jax.numpy as jnp
from jax import lax
from jax.experimental import pallas as pl
from jax.experimental.pallas import tpu as pltpu

HEAD_DIM = 64
N_HEADS = 4
RET_DV = 128
GROUP_W = N_HEADS * HEAD_DIM
RET_VW = N_HEADS * RET_DV
WINDOWS = (128, 512, 2048)
DILATIONS = (1, 4, 16)
ROPE_THETA = 10000.0
LN_EPS = 1e-5
GN_EPS = 1e-6
FFN_RES = 0.5
N_MOD = 9
LOG_GAMMA = tuple(math.log1p(-(2.0 ** (-5 - h))) for h in range(N_HEADS))

V7X_LANES = 128
N_SLABS = GROUP_W // V7X_LANES
V7X_VMEM_LIMIT_BYTES = 56 * 1024 * 1024
MOD_COLS = 2304
ROW_TILE = 512
FFN_ROW_TILE = 1024
WEIGHT_CAST_STEPS = 16
FFN_CHUNKS = 11
RET_BLOCK = 256
RET_CHUNKS_PER_STEP = 8
DIL_TILE = 128
DIL_BLOCK = DIL_TILE * max(DILATIONS)
SB_TILE = 256
SB_TILES_PER_STEP = 4
MASKED = -1e30
SB_STOP = -110.0

F32 = jnp.float32
BF16 = jnp.bfloat16


def _params(*sem):
    return pltpu.CompilerParams(dimension_semantics=sem, vmem_limit_bytes=V7X_VMEM_LIMIT_BYTES)


def _resident(shape):
    return pl.BlockSpec(shape, lambda *_: (0,) * len(shape), pipeline_mode=pl.Buffered(1))


def _layer_block(tail, layer):
    return pl.BlockSpec((None,) + tuple(tail), lambda *_: (layer,) + (0,) * len(tail), pipeline_mode=pl.Buffered(1))


def _dot(a, b):
    return jnp.dot(a, b, preferred_element_type=F32)


def _dot_nt(a, b):
    return lax.dot_general(a, b, (((1,), (1,)), ((), ())), preferred_element_type=F32)


def _silu(v):
    return v * jax.nn.sigmoid(v)


def _post_norm(x, y, gate, res_w, gain, bias, alpha):
    z = alpha * x + (res_w * (1.0 + gate)) * y
    mu = jnp.mean(z, axis=-1, keepdims=True)
    zc = z - mu
    var = jnp.mean(zc * zc, axis=-1, keepdims=True)
    return zc * lax.rsqrt(var + LN_EPS) * gain + bias


def _mod_kernel(c_ref, w_ref, b_ref, o_ref):
    c = c_ref[...]
    o_ref[...] = jnp.sum(_silu(c) * w_ref[...], axis=0, keepdims=True) + b_ref[...]


def _modulation(c, w_ada, b_ada):
    depth, d, n = w_ada.shape
    tn = MOD_COLS
    assert n % tn == 0
    out = pl.pallas_call(
        _mod_kernel,
        out_shape=jax.ShapeDtypeStruct((depth, 1, n), F32),
        grid=(depth, n // tn),
        in_specs=[
            pl.BlockSpec((d, 1), lambda l, j: (0, 0)),
            pl.BlockSpec((None, d, tn), lambda l, j: (l, 0, j)),
            pl.BlockSpec((None, 1, tn), lambda l, j: (l, 0, j)),
        ],
        out_specs=pl.BlockSpec((None, 1, tn), lambda l, j: (l, 0, j)),
        compiler_params=_params("parallel", "parallel"),
        name="adaln_mod",
    )(c.reshape(d, 1), w_ada, b_ada.reshape(depth, 1, n))
    return out.reshape(depth, N_MOD, d)


def _mod_rows(mod_ref, sub):
    return tuple(mod_ref[3 * sub + i:3 * sub + i + 1, :] for i in range(3))


def _ln_rows(ln_ref, sub):
    return ln_ref[2 * sub:2 * sub + 1, :], ln_ref[2 * sub + 1:2 * sub + 2, :]


def _ffn_postnorm(x, sub, mod_ref, ln_ref, wg_ref, wu_ref, wd_ref, alpha):
    shift, scale, gate = _mod_rows(mod_ref, sub)
    d_ff = wg_ref.shape[1]
    fc = d_ff // FFN_CHUNKS
    h = (x * (1.0 + scale) + shift).astype(BF16)
    y = jnp.zeros(x.shape, F32)
    for ci in range(FFN_CHUNKS):
        cols = slice(ci * fc, (ci + 1) * fc)
        act = _silu(_dot(h, wg_ref[:, cols])) * _dot(h, wu_ref[:, cols])
        y = y + _dot(act.astype(BF16), wd_ref[cols, :])
    return _post_norm(x, y, gate, FFN_RES, *_ln_rows(ln_ref, sub), alpha)


def _cast_specs(stack, layer, grid_steps):
    _, r, c = stack.shape
    steps = min(grid_steps, WEIGHT_CAST_STEPS)
    chunk = r // steps
    assert r % steps == 0 and chunk % 16 == 0
    last = steps - 1
    return (pl.BlockSpec((None, chunk, c), lambda i: (layer, jnp.minimum(i, last), 0)),
            pl.BlockSpec((chunk, c), lambda i: (jnp.minimum(i, last), 0)),
            jax.ShapeDtypeStruct((r, c), BF16))


def _cast_chunks(src_refs, dst_refs):
    for src, dst in zip(src_refs, dst_refs):
        dst[...] = src[...].astype(dst.dtype)


def _ffn_kernel(x_ref, mod_ref, ln_ref, wg_ref, wu_ref, wd_ref, *refs, alpha):
    n_cast = len(refs) // 2
    o_ref = refs[n_cast]
    o_ref[...] = _ffn_postnorm(x_ref[...], 0, mod_ref, ln_ref, wg_ref, wu_ref, wd_ref, alpha)
    _cast_chunks(refs[:n_cast], refs[n_cast + 1:])


def _mixout_ffn_kernel(x_ref, yr_ref, yd_ref, ys_ref, mod_ref, ln_ref, wo_ref, wg_ref, wu_ref, wd_ref, o_ref,
                       *, alpha):
    vw, g = RET_VW, GROUP_W
    y = (_dot(yr_ref[...], wo_ref[0:vw, :]) + _dot(yd_ref[...], wo_ref[vw:vw + g, :])
         + _dot(ys_ref[...], wo_ref[vw + g:vw + 2 * g, :]))
    x1 = _post_norm(x_ref[...], y, _mod_rows(mod_ref, 1)[2], 1.0, *_ln_rows(ln_ref, 1), alpha)
    o_ref[...] = _ffn_postnorm(x1, 2, mod_ref, ln_ref, wg_ref, wu_ref, wd_ref, alpha)


def _ffn_specs(d, d_ff):
    assert d_ff % (FFN_CHUNKS * V7X_LANES) == 0
    return [_resident((d, d_ff)), _resident((d, d_ff)), _resident((d_ff, d))]


def _rows(tile, width):
    return pl.BlockSpec((tile, width), lambda i: (i, 0))


def _ffn_sublayer(x, layer, mod, ln, weights, cast_stacks, alpha):
    s, d = x.shape
    assert s % FFN_ROW_TILE == 0
    casts = [_cast_specs(w, layer, s // FFN_ROW_TILE) for w in cast_stacks]
    outs = pl.pallas_call(
        functools.partial(_ffn_kernel, alpha=alpha),
        out_shape=[jax.ShapeDtypeStruct((s, d), F32)] + [c[2] for c in casts],
        grid=(s // FFN_ROW_TILE,),
        in_specs=[_rows(FFN_ROW_TILE, d), _layer_block((N_MOD, d), layer), _layer_block((6, d), layer)]
        + _ffn_specs(d, weights[0].shape[1]) + [c[0] for c in casts],
        out_specs=[_rows(FFN_ROW_TILE, d)] + [c[1] for c in casts],
        compiler_params=_params("parallel"),
        name="ffn_postnorm",
    )(x, mod, ln, *weights, *cast_stacks)
    return outs[0], outs[1:]


def _mixout_ffn_sublayer(x, y_ret, y_dil, y_sb, layer, mod, ln, w_out, weights, alpha):
    s, d = x.shape
    assert s % FFN_ROW_TILE == 0
    t = FFN_ROW_TILE
    return pl.pallas_call(
        functools.partial(_mixout_ffn_kernel, alpha=alpha),
        out_shape=jax.ShapeDtypeStruct((s, d), F32),
        grid=(s // t,),
        in_specs=[_rows(t, d), _rows(t, RET_VW), _rows(t, GROUP_W), _rows(t, GROUP_W),
                  _layer_block((N_MOD, d), layer), _layer_block((6, d), layer), _layer_block(w_out.shape[1:], layer)]
        + _ffn_specs(d, weights[0].shape[1]),
        out_specs=_rows(t, d),
        compiler_params=_params("parallel"),
        name="mixout_ffn_postnorm",
    )(x, y_ret, y_dil, y_sb, mod, ln, w_out, *weights)


def _rope(v, cos, sin):
    lane = lax.broadcasted_iota(jnp.int32, cos.shape, 1)
    first_half = (lane & (HEAD_DIM - 1)) < (HEAD_DIM // 2)
    halves = []
    for hf in range(GROUP_W // V7X_LANES):
        vh = v[:, hf * V7X_LANES:(hf + 1) * V7X_LANES]
        partner = jnp.where(first_half,
                            pltpu.roll(vh, V7X_LANES - HEAD_DIM // 2, 1),
                            pltpu.roll(vh, HEAD_DIM // 2, 1))
        halves.append(vh * cos + partner * sin)
    return jnp.concatenate(halves, axis=1)


def _tile_rope(within_ref, start_ref):
    cos_r, sin_r = within_ref[0], within_ref[1]
    c0, s0, c0s, s0s = (start_ref[i:i + 1, :] for i in range(4))
    return c0 * cos_r - s0 * sin_r, s0s * cos_r + c0s * sin_r


N_INPROJ_IN, N_INPROJ_OUT = 7, 10


def _inproj_kernel(*refs):
    n_cast = (len(refs) - N_INPROJ_IN - N_INPROJ_OUT) // 2
    x_ref, mod_ref, w_ref, rw_ref, rs_ref, dw_ref, ds_ref = refs[:N_INPROJ_IN]
    out_refs = refs[N_INPROJ_IN + n_cast:N_INPROJ_IN + n_cast + N_INPROJ_OUT]
    rq_ref, rk_ref, rv_ref, rg_ref, dq_ref, dk_ref, dv_ref, sq_ref, sk_ref, sv_ref = out_refs
    _cast_chunks(refs[N_INPROJ_IN:N_INPROJ_IN + n_cast], refs[N_INPROJ_IN + n_cast + N_INPROJ_OUT:])
    shift, scale, _ = _mod_rows(mod_ref, 1)
    h = (x_ref[...] * (1.0 + scale) + shift).astype(BF16)
    qk_scale = HEAD_DIM ** -0.5
    g, vw = GROUP_W, RET_VW

    full = _dot(h, w_ref[...])

    def proj(start, width):
        return full[:, start:start + width]

    cr, sr = _tile_rope(rw_ref, rs_ref)
    cd, sd = _tile_rope(dw_ref, ds_ref)
    rq_ref[...] = _rope(proj(0, g), cr, sr).astype(BF16)
    rk_ref[...] = (_rope(proj(g, g), cr, sr) * qk_scale).astype(BF16)
    rv_ref[...] = proj(2 * g, vw).astype(BF16)
    rg_ref[...] = proj(2 * g + vw, vw)
    base = 2 * g + 2 * vw
    for ref, val in ((dq_ref, _rope(proj(base, g), cd, sd) * qk_scale),
                     (dk_ref, _rope(proj(base + g, g), cd, sd)), (dv_ref, proj(base + 2 * g, g))):
        for sl in range(N_SLABS):
            ref[sl] = val[:, sl * V7X_LANES:(sl + 1) * V7X_LANES]
    sq_ref[...] = (proj(base + 3 * g, g) * qk_scale).astype(BF16)
    sk_ref[...] = proj(base + 4 * g, g).astype(BF16)
    sv_ref[...] = proj(base + 5 * g, g).astype(BF16)


def _in_projection(x, layer, mod, w_in, tables, cast_stacks, cast_layer):
    s, d = x.shape
    n = w_in.shape[2]
    g, vw = GROUP_W, RET_VW
    assert n == 8 * g + 2 * vw

    def rows(width):
        return _rows(ROW_TILE, width)

    def flat(width, dtype):
        return jax.ShapeDtypeStruct((s, width), dtype), rows(width)

    slabs = (jax.ShapeDtypeStruct((N_SLABS, s, V7X_LANES), F32),
             pl.BlockSpec((N_SLABS, ROW_TILE, V7X_LANES), lambda i: (0, i, 0)))
    outs = (flat(g, BF16), flat(g, BF16), flat(vw, BF16), flat(vw, F32), slabs, slabs, slabs,
            flat(g, BF16), flat(g, BF16), flat(g, BF16))
    assert len(outs) == N_INPROJ_OUT
    casts = [_cast_specs(w, cast_layer, s // ROW_TILE) for w in cast_stacks]
    return pl.pallas_call(
        _inproj_kernel,
        out_shape=[o[0] for o in outs] + [c[2] for c in casts],
        grid=(s // ROW_TILE,),
        in_specs=[rows(d), _layer_block((N_MOD, d), layer), _layer_block((d, n), layer)]
        + [_resident((2, ROW_TILE, V7X_LANES)), pl.BlockSpec((None, 4, V7X_LANES), lambda i: (i, 0, 0))] * 2
        + [c[0] for c in casts],
        out_specs=[o[1] for o in outs] + [c[1] for c in casts],
        compiler_params=_params("parallel"),
        name="mixer_inproj",
    )(x, mod, w_in, *tables, *cast_stacks)


def _rope_tables(inv_freq, s):
    half = HEAD_DIM // 2
    lane_freq = jnp.tile(inv_freq, V7X_LANES // half)[None, :]
    sign = jnp.tile(jnp.concatenate([-jnp.ones(half, F32), jnp.ones(half, F32)]), V7X_LANES // HEAD_DIM)[None, :]
    within = jnp.arange(ROW_TILE, dtype=F32)[:, None] * lane_freq
    start = (jnp.arange(s // ROW_TILE, dtype=F32) * ROW_TILE)[:, None] * lane_freq
    c0, s0 = jnp.cos(start), jnp.sin(start)
    return (jnp.stack([jnp.cos(within), jnp.sin(within)]), jnp.stack([c0, s0, sign * c0, sign * s0], axis=1))


def _head_of(idx, width):
    assert width & (width - 1) == 0
    return lax.shift_right_logical(idx, width.bit_length() - 1)


def _mod_pow2(idx, width):
    assert width & (width - 1) == 0
    return idx & (width - 1)


def _log_gamma_of(head):
    lg = jnp.full(head.shape, LOG_GAMMA[N_HEADS - 1], F32)
    for hh in range(N_HEADS - 2, -1, -1):
        lg = jnp.where(head == hh, LOG_GAMMA[hh], lg)
    return lg


def _own_head_lanes(rows):
    r = lax.broadcasted_iota(jnp.int32, (N_HEADS * rows, GROUP_W), 0)
    l = lax.broadcasted_iota(jnp.int32, (N_HEADS * rows, GROUP_W), 1)
    return _head_of(r, rows) == _head_of(l, HEAD_DIM)


def _stack_heads(q, rows):
    qf = q.astype(F32)
    return jnp.where(_own_head_lanes(rows), jnp.concatenate([qf] * N_HEADS, axis=0), 0.0).astype(q.dtype)


def _merge_heads(stacked, rows):
    l = lax.broadcasted_iota(jnp.int32, (rows, GROUP_W), 1)
    head = _head_of(l, HEAD_DIM)
    out = jnp.zeros((rows, GROUP_W), stacked.dtype)
    for hh in range(N_HEADS):
        out = jnp.where(head == hh, stacked[hh * rows:(hh + 1) * rows, :], out)
    return out


def _ret_kernel(q_ref, k_ref, v_ref, g_ref, o_ref, state_ref, decay_ref, xi_ref, zeta_ref, gc_ref):
    c = RET_BLOCK
    rows = N_HEADS * c

    @pl.when(pl.program_id(0) == 0)
    def _init():
        state_ref[...] = jnp.zeros(state_ref.shape, F32)
        r = lax.broadcasted_iota(jnp.int32, (rows, c), 0)
        j = lax.broadcasted_iota(jnp.int32, (rows, c), 1)
        diff = (_mod_pow2(r, c) - j).astype(F32)
        lg = _log_gamma_of(_head_of(r, c))
        decay_ref[...] = jnp.where(diff >= 0, jnp.exp(lg * jnp.maximum(diff, 0.0)), 0.0)
        r = lax.broadcasted_iota(jnp.int32, (rows, RET_DV), 0)
        xi_ref[...] = jnp.exp(_log_gamma_of(_head_of(r, c)) * (_mod_pow2(r, c).astype(F32) + 1.0))
        i = lax.broadcasted_iota(jnp.int32, (c, GROUP_W), 0)
        l = lax.broadcasted_iota(jnp.int32, (c, GROUP_W), 1)
        zeta_ref[...] = jnp.exp(_log_gamma_of(_head_of(l, HEAD_DIM)) * (c - 1.0 - i.astype(F32)))
        r = lax.broadcasted_iota(jnp.int32, (GROUP_W, RET_DV), 0)
        gc_ref[...] = jnp.exp(_log_gamma_of(_head_of(r, HEAD_DIM)) * float(c))

    state = state_ref[...]
    gc = gc_ref[...]
    for ci in range(RET_CHUNKS_PER_STEP):
        rsl = slice(ci * c, (ci + 1) * c)
        q, k, v = q_ref[rsl, :], k_ref[rsl, :], v_ref[rsl, :]
        qs = _stack_heads(q, c)
        scores = (_dot_nt(qs, k) * decay_ref[...]).astype(BF16)
        inner = jnp.concatenate(
            [_dot(scores[hh * c:(hh + 1) * c, :], v[:, hh * RET_DV:(hh + 1) * RET_DV]) for hh in range(N_HEADS)],
            axis=0)
        o = inner + _dot(qs, state.astype(BF16)) * xi_ref[...]
        mu = jnp.mean(o, axis=-1, keepdims=True)
        oc = o - mu
        var = jnp.mean(oc * oc, axis=-1, keepdims=True)
        on = oc * lax.rsqrt(var + GN_EPS)
        for hh in range(N_HEADS):
            lanes = slice(hh * RET_DV, (hh + 1) * RET_DV)
            o_ref[rsl, lanes] = (_silu(g_ref[rsl, lanes]) * on[hh * c:(hh + 1) * c, :]).astype(o_ref.dtype)
        kz_t = (k.astype(F32) * zeta_ref[...]).T.astype(BF16)
        kv = _dot(kz_t, v)
        state = jnp.concatenate(
            [state[hh * HEAD_DIM:(hh + 1) * HEAD_DIM, :] * gc[hh * HEAD_DIM:(hh + 1) * HEAD_DIM, :]
             + kv[hh * HEAD_DIM:(hh + 1) * HEAD_DIM, hh * RET_DV:(hh + 1) * RET_DV] for hh in range(N_HEADS)], axis=0)
    state_ref[...] = state


def _retention(rq, rk, rv, rg):
    s = rq.shape[0]
    c = RET_BLOCK
    step_rows = c * RET_CHUNKS_PER_STEP
    assert s % step_rows == 0

    def rows(width):
        return _rows(step_rows, width)

    return pl.pallas_call(
        _ret_kernel,
        out_shape=jax.ShapeDtypeStruct((s, RET_VW), BF16),
        grid=(s // step_rows,),
        in_specs=[rows(GROUP_W), rows(GROUP_W), rows(RET_VW), rows(RET_VW)],
        out_specs=rows(RET_VW),
        scratch_shapes=[pltpu.VMEM((GROUP_W, RET_DV), F32), pltpu.VMEM((N_HEADS * c, c), F32),
                        pltpu.VMEM((N_HEADS * c, RET_DV), F32), pltpu.VMEM((c, GROUP_W), F32),
                        pltpu.VMEM((GROUP_W, RET_DV), F32)],
        compiler_params=_params("arbitrary"),
        name="retention",
    )(rq, rk, rv, rg)


def _load_rows(ref, start, rows, stride):
    idx = pl.ds(start, rows) if stride == 1 else pl.ds(start, rows, stride=stride)
    return jnp.concatenate([ref[sl, idx, :] for sl in range(N_SLABS)], axis=1)


def _store_rows(ref, start, rows, stride, val):
    idx = pl.ds(start, rows) if stride == 1 else pl.ds(start, rows, stride=stride)
    for sl in range(N_SLABS):
        ref[sl, idx, :] = val[:, sl * V7X_LANES:(sl + 1) * V7X_LANES]


def _dil_kernel(q_ref, k_ref, v_ref, o_ref, kwin_ref, vwin_ref, part_o_ref, part_l_ref, const_ref):
    t, blk = DIL_TILE, DIL_BLOCK
    b = pl.program_id(0)

    @pl.when(b == 0)
    def _first():
        kwin_ref[:, 0:blk, :] = jnp.zeros((N_SLABS, blk, V7X_LANES), F32)
        vwin_ref[:, 0:blk, :] = jnp.zeros((N_SLABS, blk, V7X_LANES), F32)

    @pl.when(b > 0)
    def _shift():
        kwin_ref[:, 0:blk, :] = kwin_ref[:, blk:2 * blk, :]
        vwin_ref[:, 0:blk, :] = vwin_ref[:, blk:2 * blk, :]

    kwin_ref[:, blk:2 * blk, :] = k_ref[...]
    vwin_ref[:, blk:2 * blk, :] = v_ref[...]

    rows = N_HEADS * t
    qi = _mod_pow2(lax.broadcasted_iota(jnp.int32, (rows, 2 * t), 0), t)
    kc = lax.broadcasted_iota(jnp.int32, (rows, 2 * t), 1)
    ahead = kc - qi
    band = jnp.where(ahead >= 0, jnp.where(ahead <= t, 0.0, MASKED), MASKED)
    const_ref[0] = jnp.where(_own_head_lanes(t), 1.0, 0.0)
    const_ref[1] = band
    const_ref[2] = jnp.where(kc >= t, band, MASKED)

    for pi, dil in enumerate(DILATIONS):

        def unit(u, carry, pi=pi, dil=dil):
            res = _mod_pow2(u, dil)
            ct = lax.shift_right_logical(u, dil.bit_length() - 1)
            q_start = res + ct * (dil * t)
            k_start = blk + q_start - dil * t
            mask_id = jnp.where(jnp.logical_and(b == 0, ct == 0), 2, 1)
            qf = _load_rows(q_ref, q_start, t, dil)
            qs = (jnp.concatenate([qf] * N_HEADS, axis=0) * const_ref[0]).astype(BF16)
            kk = _load_rows(kwin_ref, k_start, 2 * t, dil).astype(BF16)
            vv = _load_rows(vwin_ref, k_start, 2 * t, dil).astype(BF16)
            z = _dot_nt(qs, kk) + const_ref[mask_id]
            m = jnp.max(z, axis=-1, keepdims=True)
            p = jnp.exp(z - m)
            l = jnp.sum(p, axis=-1, keepdims=True)
            o = _dot(p.astype(BF16), vv) / l
            lse = jnp.broadcast_to(m + jnp.log(l), (rows, GROUP_W))
            _store_rows(part_o_ref.at[pi], q_start, t, dil, _merge_heads(o, t))
            _store_rows(part_l_ref.at[pi], q_start, t, dil, _merge_heads(lse, t))
            return carry

        lax.fori_loop(0, blk // t, unit, 0, unroll=8)

    def combine(ci, carry):
        r0 = pl.multiple_of(ci * t, t)
        for sl in range(N_SLABS):
            ls = [part_l_ref[pi, sl, pl.ds(r0, t), :] for pi in range(len(DILATIONS))]
            top = functools.reduce(jnp.maximum, ls)
            ws = [jnp.exp(x - top) for x in ls]
            num = sum(w * part_o_ref[pi, sl, pl.ds(r0, t), :] for pi, w in enumerate(ws))
            o_ref[pl.ds(r0, t), sl * V7X_LANES:(sl + 1) * V7X_LANES] = (num / sum(ws)).astype(o_ref.dtype)
        return carry

    lax.fori_loop(0, blk // t, combine, 0)


def _dilated_attention(dq, dk, dv):
    s = dq.shape[1]
    blk = DIL_BLOCK
    assert s % blk == 0 and all(w == DIL_TILE * d for w, d in zip(WINDOWS, DILATIONS))
    assert 2 * DIL_TILE == GROUP_W
    slab = pl.BlockSpec((N_SLABS, blk, V7X_LANES), lambda i: (0, i, 0))
    n_pat = len(DILATIONS)
    return pl.pallas_call(
        _dil_kernel,
        out_shape=jax.ShapeDtypeStruct((s, GROUP_W), BF16),
        grid=(s // blk,),
        in_specs=[slab, slab, slab],
        out_specs=pl.BlockSpec((blk, GROUP_W), lambda i: (i, 0)),
        scratch_shapes=[pltpu.VMEM((N_SLABS, 2 * blk, V7X_LANES), F32), pltpu.VMEM((N_SLABS, 2 * blk, V7X_LANES), F32),
                        pltpu.VMEM((n_pat, N_SLABS, blk, V7X_LANES), F32),
                        pltpu.VMEM((n_pat, N_SLABS, blk, V7X_LANES), F32),
                        pltpu.VMEM((3, N_HEADS * DIL_TILE, 2 * DIL_TILE), F32)],
        compiler_params=_params("arbitrary"),
        name="dilated_attn",
    )(dq, dk, dv)


def _sb_kernel(q_ref, k_ref, v_ref, o_ref, acc_ref, carry_ref):
    t = SB_TILE
    rows = N_HEADS * t
    r = lax.broadcasted_iota(jnp.int32, (rows, t), 0)
    col = lax.broadcasted_iota(jnp.int32, (rows, t), 1)
    causal = _mod_pow2(r, t) > col
    kj = lax.broadcasted_iota(jnp.int32, (t, t), 0)
    ks = lax.broadcasted_iota(jnp.int32, (t, t), 1)
    later = jnp.where(kj > ks, 1.0, 0.0).astype(BF16)

    def visit(qi, qs, jj, carry, on_diagonal=False, live=None):
        k0 = pl.multiple_of(jnp.maximum(qi - jj, 0) * t, t)
        z = _dot_nt(qs, k_ref[pl.ds(k0, t), :])
        softplus = jnp.maximum(z, 0.0) + jnp.log(1.0 + jnp.exp(-jnp.abs(z)))
        log_keep = -softplus
        if on_diagonal:
            log_keep = jnp.where(causal, log_keep, 0.0)
        if live is not None:
            log_keep = log_keep * live
        within = _dot(log_keep.astype(BF16), later)
        a = jnp.exp((z - softplus) + within + carry)
        if on_diagonal:
            a = jnp.where(causal, a, 0.0)
        if live is not None:
            a = a * live
        return _dot(a.astype(BF16), v_ref[pl.ds(k0, t), :]), carry + jnp.sum(log_keep, axis=-1, keepdims=True)

    pending = []
    for bi in range(SB_TILES_PER_STEP):
        qi = pl.program_id(0) * SB_TILES_PER_STEP + bi
        qs = _stack_heads(q_ref[bi * t:(bi + 1) * t, :], t)
        out0, carry0 = visit(qi, qs, 0, jnp.zeros((rows, 1), F32), on_diagonal=True)
        out1, carry1 = visit(qi, qs, 1, carry0, live=jnp.where(qi >= 1, 1.0, 0.0))
        acc_ref[bi] = out0 + out1
        carry_ref[bi] = carry1
        pending.append((qi, qs, jnp.max(carry1)))

    for bi, (qi, qs, carry_max) in enumerate(pending):

        def unfinished(state, qi=qi):
            jj, carry_max = state
            return jnp.logical_and(jj <= qi, carry_max > SB_STOP)

        def sweep(state, bi=bi, qi=qi, qs=qs):
            jj, _ = state
            out, carry = visit(qi, qs, jj, carry_ref[bi])
            acc_ref[bi] += out
            carry_ref[bi] = carry
            return jj + 1, jnp.max(carry)

        lax.while_loop(unfinished, sweep, (jnp.int32(2), carry_max))
        o_ref[bi * t:(bi + 1) * t, :] = _merge_heads(acc_ref[bi], t).astype(o_ref.dtype)


def _stick_breaking(sq, sk, sv):
    s = sq.shape[0]
    t = SB_TILE
    step_rows = t * SB_TILES_PER_STEP
    assert s % step_rows == 0
    rows = _rows(step_rows, GROUP_W)
    return pl.pallas_call(
        _sb_kernel,
        out_shape=jax.ShapeDtypeStruct((s, GROUP_W), BF16),
        grid=(s // step_rows,),
        in_specs=[rows, _resident((s, GROUP_W)), _resident((s, GROUP_W))],
        out_specs=rows,
        scratch_shapes=[pltpu.VMEM((SB_TILES_PER_STEP, N_HEADS * t, GROUP_W), F32),
                        pltpu.VMEM((SB_TILES_PER_STEP, N_HEADS * t, 1), F32)],
        compiler_params=_params("parallel"),
        name="stickbreak_attn",
    )(sq, sk, sv)


def kernel(x, c, w_ada, b_ada, ln_gain, ln_bias, ffn1_w_gate, ffn1_w_up, ffn1_w_down,
           w_in, w_out, ffn2_w_gate, ffn2_w_up, ffn2_w_down):
    batch, s, d = x.shape
    depth = w_ada.shape[0]
    assert batch == 1 and c.shape == (1, d)
    alpha = (2.0 * depth) ** 0.25
    half = HEAD_DIM // 2
    ret_freq = 1.0 / (ROPE_THETA ** jnp.linspace(0.0, 1.0, half, dtype=F32))
    rope_freq = 1.0 / (ROPE_THETA ** (jnp.arange(0, HEAD_DIM, 2, dtype=F32) / HEAD_DIM))
    tables = _rope_tables(ret_freq, s) + _rope_tables(rope_freq, s)
    mod = _modulation(c, w_ada, b_ada)
    ln = jnp.stack([ln_gain, ln_bias], axis=2).reshape(depth, 6, d)
    win, wout = w_in.astype(BF16), w_out.astype(BF16)
    ffn1_stacks = (ffn1_w_gate, ffn1_w_up, ffn1_w_down)
    ffn2_stacks = (ffn2_w_gate, ffn2_w_up, ffn2_w_down)
    ffn1 = [w[0].astype(BF16) for w in ffn1_stacks]
    xs = x[0]
    for l in range(depth):
        xs, ffn2 = _ffn_sublayer(xs, l, mod, ln, ffn1, ffn2_stacks, alpha)
        nxt = ffn1_stacks if l + 1 < depth else ()
        projected = _in_projection(xs, l, mod, win, tables, nxt, l + 1)
        rq, rk, rv, rg, dq, dk, dv, sq, sk, sv = projected[:N_INPROJ_OUT]
        ffn1 = projected[N_INPROJ_OUT:]
        y_ret = _retention(rq, rk, rv, rg)
        y_dil = _dilated_attention(dq, dk, dv)
        y_sb = _stick_breaking(sq, sk, sv)
        xs = _mixout_ffn_sublayer(xs, y_ret, y_dil, y_sb, l, mod, ln, wout, ffn2, alpha)
    return xs[None]
```

```python
import functools
import math

import jax
import jax.numpy as jnp
from jax import lax
from jax.experimental import pallas as pl
from jax.experimental.pallas import tpu as pltpu

HEAD_DIM = 64
N_HEADS = 4
RET_DV = 128
GROUP_W = N_HEADS * HEAD_DIM
RET_VW = N_HEADS * RET_DV
WINDOWS = (128, 512, 2048)
DILATIONS = (1, 4, 16)
ROPE_THETA = 10000.0
LN_EPS = 1e-5
GN_EPS = 1e-6
FFN_RES = 0.5
N_MOD = 9
LOG_GAMMA = tuple(math.log1p(-(2.0 ** (-5 - h))) for h in range(N_HEADS))

V7X_LANES = 128
N_SLABS = GROUP_W // V7X_LANES
V7X_VMEM_LIMIT_BYTES = 56 * 1024 * 1024
MOD_COLS = 2304
ROW_TILE = 512
FFN_ROW_TILE = 1024
WEIGHT_CAST_STEPS = 16
FFN_CHUNKS = 11
RET_BLOCK = 256
RET_CHUNKS_PER_STEP = 8
DIL_TILE = 128
DIL_BLOCK = DIL_TILE * max(DILATIONS)
SB_TILE = 256
SB_TILES_PER_STEP = 4
MASKED = -1e30
SB_STOP = -110.0

F32 = jnp.float32
BF16 = jnp.bfloat16


def _params(*sem):
    return pltpu.CompilerParams(dimension_semantics=sem, vmem_limit_bytes=V7X_VMEM_LIMIT_BYTES)


def _resident(shape):
    return pl.BlockSpec(shape, lambda *_: (0,) * len(shape), pipeline_mode=pl.Buffered(1))


def _layer_block(tail, layer):
    return pl.BlockSpec((None,) + tuple(tail), lambda *_: (layer,) + (0,) * len(tail), pipeline_mode=pl.Buffered(1))


def _dot(a, b):
    return jnp.dot(a, b, preferred_element_type=F32)


def _dot_nt(a, b):
    return lax.dot_general(a, b, (((1,), (1,)), ((), ())), preferred_element_type=F32)


def _silu(v):
    return v * jax.nn.sigmoid(v)


def _post_norm(x, y, gate, res_w, gain, bias, alpha):
    z = alpha * x + (res_w * (1.0 + gate)) * y
    mu = jnp.mean(z, axis=-1, keepdims=True)
    zc = z - mu
    var = jnp.mean(zc * zc, axis=-1, keepdims=True)
    return zc * lax.rsqrt(var + LN_EPS) * gain + bias


def _mod_kernel(c_ref, w_ref, b_ref, o_ref):
    c = c_ref[...]
    o_ref[...] = jnp.sum(_silu(c) * w_ref[...], axis=0, keepdims=True) + b_ref[...]


def _modulation(c, w_ada, b_ada):
    depth, d, n = w_ada.shape
    tn = MOD_COLS
    assert n % tn == 0
    out = pl.pallas_call(
        _mod_kernel,
        out_shape=jax.ShapeDtypeStruct((depth, 1, n), F32),
        grid=(depth, n // tn),
        in_specs=[
            pl.BlockSpec((d, 1), lambda l, j: (0, 0)),
            pl.BlockSpec((None, d, tn), lambda l, j: (l, 0, j)),
            pl.BlockSpec((None, 1, tn), lambda l, j: (l, 0, j)),
        ],
        out_specs=pl.BlockSpec((None, 1, tn), lambda l, j: (l, 0, j)),
        compiler_params=_params("parallel", "parallel"),
        name="adaln_mod",
    )(c.reshape(d, 1), w_ada, b_ada.reshape(depth, 1, n))
    return out.reshape(depth, N_MOD, d)


def _mod_rows(mod_ref, sub):
    return tuple(mod_ref[3 * sub + i:3 * sub + i + 1, :] for i in range(3))


def _ln_rows(ln_ref, sub):
    return ln_ref[2 * sub:2 * sub + 1, :], ln_ref[2 * sub + 1:2 * sub + 2, :]


def _ffn_postnorm(x, sub, mod_ref, ln_ref, wg_ref, wu_ref, wd_ref, alpha):
    shift, scale, gate = _mod_rows(mod_ref, sub)
    d_ff = wg_ref.shape[1]
    fc = d_ff // FFN_CHUNKS
    h = (x * (1.0 + scale) + shift).astype(BF16)
    y = jnp.zeros(x.shape, F32)
    for ci in range(FFN_CHUNKS):
        cols = slice(ci * fc, (ci + 1) * fc)
        act = _silu(_dot(h, wg_ref[:, cols])) * _dot(h, wu_ref[:, cols])
        y = y + _dot(act.astype(BF16), wd_ref[cols, :])
    return _post_norm(x, y, gate, FFN_RES, *_ln_rows(ln_ref, sub), alpha)


def _cast_specs(stack, layer, grid_steps):
    _, r, c = stack.shape
    steps = min(grid_steps, WEIGHT_CAST_STEPS)
    chunk = r // steps
    assert r % steps == 0 and chunk % 16 == 0
    last = steps - 1
    return (pl.BlockSpec((None, chunk, c), lambda i: (layer, jnp.minimum(i, last), 0)),
            pl.BlockSpec((chunk, c), lambda i: (jnp.minimum(i, last), 0)),
            jax.ShapeDtypeStruct((r, c), BF16))


def _cast_chunks(src_refs, dst_refs):
    for src, dst in zip(src_refs, dst_refs):
        dst[...] = src[...].astype(dst.dtype)


def _ffn_kernel(x_ref, mod_ref, ln_ref, wg_ref, wu_ref, wd_ref, *refs, alpha):
    n_cast = len(refs) // 2
    o_ref = refs[n_cast]
    o_ref[...] = _ffn_postnorm(x_ref[...], 0, mod_ref, ln_ref, wg_ref, wu_ref, wd_ref, alpha)
    _cast_chunks(refs[:n_cast], refs[n_cast + 1:])


def _mixout_ffn_kernel(x_ref, yr_ref, yd_ref, ys_ref, mod_ref, ln_ref, wo_ref, wg_ref, wu_ref, wd_ref, o_ref,
                       *, alpha):
    vw, g = RET_VW, GROUP_W
    y = (_dot(yr_ref[...], wo_ref[0:vw, :]) + _dot(yd_ref[...], wo_ref[vw:vw + g, :])
         + _dot(ys_ref[...], wo_ref[vw + g:vw + 2 * g, :]))
    x1 = _post_norm(x_ref[...], y, _mod_rows(mod_ref, 1)[2], 1.0, *_ln_rows(ln_ref, 1), alpha)
    o_ref[...] = _ffn_postnorm(x1, 2, mod_ref, ln_ref, wg_ref, wu_ref, wd_ref, alpha)


def _ffn_specs(d, d_ff):
    assert d_ff % (FFN_CHUNKS * V7X_LANES) == 0
    return [_resident((d, d_ff)), _resident((d, d_ff)), _resident((d_ff, d))]


def _rows(tile, width):
    return pl.BlockSpec((tile, width), lambda i: (i, 0))


def _ffn_sublayer(x, layer, mod, ln, weights, cast_stacks, alpha):
    s, d = x.shape
    assert s % FFN_ROW_TILE == 0
    casts = [_cast_specs(w, layer, s // FFN_ROW_TILE) for w in cast_stacks]
    outs = pl.pallas_call(
        functools.partial(_ffn_kernel, alpha=alpha),
        out_shape=[jax.ShapeDtypeStruct((s, d), F32)] + [c[2] for c in casts],
        grid=(s // FFN_ROW_TILE,),
        in_specs=[_rows(FFN_ROW_TILE, d), _layer_block((N_MOD, d), layer), _layer_block((6, d), layer)]
        + _ffn_specs(d, weights[0].shape[1]) + [c[0] for c in casts],
        out_specs=[_rows(FFN_ROW_TILE, d)] + [c[1] for c in casts],
        compiler_params=_params("parallel"),
        name="ffn_postnorm",
    )(x, mod, ln, *weights, *cast_stacks)
    return outs[0], outs[1:]


def _mixout_ffn_sublayer(x, y_ret, y_dil, y_sb, layer, mod, ln, w_out, weights, alpha):
    s, d = x.shape
    assert s % FFN_ROW_TILE == 0
    t = FFN_ROW_TILE
    return pl.pallas_call(
        functools.partial(_mixout_ffn_kernel, alpha=alpha),
        out_shape=jax.ShapeDtypeStruct((s, d), F32),
        grid=(s // t,),
        in_specs=[_rows(t, d), _rows(t, RET_VW), _rows(t, GROUP_W), _rows(t, GROUP_W),
                  _layer_block((N_MOD, d), layer), _layer_block((6, d), layer), _resident(w_out.shape)]
        + _ffn_specs(d, weights[0].shape[1]),
        out_specs=_rows(t, d),
        compiler_params=_params("parallel"),
        name="mixout_ffn_postnorm",
    )(x, y_ret, y_dil, y_sb, mod, ln, w_out, *weights)


def _rope(v, cos, sin):
    lane = lax.broadcasted_iota(jnp.int32, cos.shape, 1)
    first_half = (lane & (HEAD_DIM - 1)) < (HEAD_DIM // 2)
    halves = []
    for hf in range(GROUP_W // V7X_LANES):
        vh = v[:, hf * V7X_LANES:(hf + 1) * V7X_LANES]
        partner = jnp.where(first_half,
                            pltpu.roll(vh, V7X_LANES - HEAD_DIM // 2, 1),
                            pltpu.roll(vh, HEAD_DIM // 2, 1))
        halves.append(vh * cos + partner * sin)
    return jnp.concatenate(halves, axis=1)


def _tile_rope(within_ref, start_ref):
    cos_r, sin_r = within_ref[0], within_ref[1]
    c0, s0, c0s, s0s = (start_ref[i:i + 1, :] for i in range(4))
    return c0 * cos_r - s0 * sin_r, s0s * cos_r + c0s * sin_r


N_INPROJ_IN, N_INPROJ_OUT = 7, 10


def _inproj_kernel(*refs):
    n_cast = (len(refs) - N_INPROJ_IN - N_INPROJ_OUT) // 2
    x_ref, mod_ref, w_ref, rw_ref, rs_ref, dw_ref, ds_ref = refs[:N_INPROJ_IN]
    out_refs = refs[N_INPROJ_IN + n_cast:N_INPROJ_IN + n_cast + N_INPROJ_OUT]
    rq_ref, rk_ref, rv_ref, rg_ref, dq_ref, dk_ref, dv_ref, sq_ref, sk_ref, sv_ref = out_refs
    _cast_chunks(refs[N_INPROJ_IN:N_INPROJ_IN + n_cast], refs[N_INPROJ_IN + n_cast + N_INPROJ_OUT:])
    shift, scale, _ = _mod_rows(mod_ref, 1)
    h = (x_ref[...] * (1.0 + scale) + shift).astype(BF16)
    qk_scale = HEAD_DIM ** -0.5
    g, vw = GROUP_W, RET_VW

    full = _dot(h, w_ref[...])

    def proj(start, width):
        return full[:, start:start + width]

    cr, sr = _tile_rope(rw_ref, rs_ref)
    cd, sd = _tile_rope(dw_ref, ds_ref)
    rq_ref[...] = _rope(proj(0, g), cr, sr).astype(BF16)
    rk_ref[...] = (_rope(proj(g, g), cr, sr) * qk_scale).astype(BF16)
    rv_ref[...] = proj(2 * g, vw).astype(BF16)
    rg_ref[...] = proj(2 * g + vw, vw)
    base = 2 * g + 2 * vw
    for ref, val in ((dq_ref, _rope(proj(base, g), cd, sd) * qk_scale),
                     (dk_ref, _rope(proj(base + g, g), cd, sd)), (dv_ref, proj(base + 2 * g, g))):
        for sl in range(N_SLABS):
            ref[sl] = val[:, sl * V7X_LANES:(sl + 1) * V7X_LANES]
    sq_ref[...] = (proj(base + 3 * g, g) * qk_scale).astype(BF16)
    sk_ref[...] = proj(base + 4 * g, g).astype(BF16)
    sv_ref[...] = proj(base + 5 * g, g).astype(BF16)


def _in_projection(x, layer, mod, w_in, tables, cast_stacks, cast_layer):
    s, d = x.shape
    n = w_in.shape[1]
    g, vw = GROUP_W, RET_VW
    assert n == 8 * g + 2 * vw

    def rows(width):
        return _rows(ROW_TILE, width)

    def flat(width, dtype):
        return jax.ShapeDtypeStruct((s, width), dtype), rows(width)

    slabs = (jax.ShapeDtypeStruct((N_SLABS, s, V7X_LANES), F32),
             pl.BlockSpec((N_SLABS, ROW_TILE, V7X_LANES), lambda i: (0, i, 0)))
    outs = (flat(g, BF16), flat(g, BF16), flat(vw, BF16), flat(vw, F32), slabs, slabs, slabs,
            flat(g, BF16), flat(g, BF16), flat(g, BF16))
    assert len(outs) == N_INPROJ_OUT
    casts = [_cast_specs(w, cast_layer, s // ROW_TILE) for w in cast_stacks]
    return pl.pallas_call(
        _inproj_kernel,
        out_shape=[o[0] for o in outs] + [c[2] for c in casts],
        grid=(s // ROW_TILE,),
        in_specs=[rows(d), _layer_block((N_MOD, d), layer), _resident((d, n))]
        + [_resident((2, ROW_TILE, V7X_LANES)), pl.BlockSpec((None, 4, V7X_LANES), lambda i: (i, 0, 0))] * 2
        + [c[0] for c in casts],
        out_specs=[o[1] for o in outs] + [c[1] for c in casts],
        compiler_params=_params("parallel"),
        name="mixer_inproj",
    )(x, mod, w_in, *tables, *cast_stacks)


def _rope_tables(inv_freq, s):
    half = HEAD_DIM // 2
    lane_freq = jnp.tile(inv_freq, V7X_LANES // half)[None, :]
    sign = jnp.tile(jnp.concatenate([-jnp.ones(half, F32), jnp.ones(half, F32)]), V7X_LANES // HEAD_DIM)[None, :]
    within = jnp.arange(ROW_TILE, dtype=F32)[:, None] * lane_freq
    start = (jnp.arange(s // ROW_TILE, dtype=F32) * ROW_TILE)[:, None] * lane_freq
    c0, s0 = jnp.cos(start), jnp.sin(start)
    return (jnp.stack([jnp.cos(within), jnp.sin(within)]), jnp.stack([c0, s0, sign * c0, sign * s0], axis=1))


def _head_of(idx, width):
    assert width & (width - 1) == 0
    return lax.shift_right_logical(idx, width.bit_length() - 1)


def _mod_pow2(idx, width):
    assert width & (width - 1) == 0
    return idx & (width - 1)


def _log_gamma_of(head):
    lg = jnp.full(head.shape, LOG_GAMMA[N_HEADS - 1], F32)
    for hh in range(N_HEADS - 2, -1, -1):
        lg = jnp.where(head == hh, LOG_GAMMA[hh], lg)
    return lg


def _own_head_lanes(rows):
    r = lax.broadcasted_iota(jnp.int32, (N_HEADS * rows, GROUP_W), 0)
    l = lax.broadcasted_iota(jnp.int32, (N_HEADS * rows, GROUP_W), 1)
    return _head_of(r, rows) == _head_of(l, HEAD_DIM)


def _stack_heads(q, rows):
    qf = q.astype(F32)
    return jnp.where(_own_head_lanes(rows), jnp.concatenate([qf] * N_HEADS, axis=0), 0.0).astype(q.dtype)


def _merge_heads(stacked, rows):
    l = lax.broadcasted_iota(jnp.int32, (rows, GROUP_W), 1)
    head = _head_of(l, HEAD_DIM)
    out = jnp.zeros((rows, GROUP_W), stacked.dtype)
    for hh in range(N_HEADS):
        out = jnp.where(head == hh, stacked[hh * rows:(hh + 1) * rows, :], out)
    return out


def _ret_kernel(q_ref, k_ref, v_ref, g_ref, o_ref, state_ref, decay_ref, xi_ref, zeta_ref, gc_ref):
    c = RET_BLOCK
    rows = N_HEADS * c

    @pl.when(pl.program_id(0) == 0)
    def _init():
        state_ref[...] = jnp.zeros(state_ref.shape, F32)
        r = lax.broadcasted_iota(jnp.int32, (rows, c), 0)
        j = lax.broadcasted_iota(jnp.int32, (rows, c), 1)
        diff = (_mod_pow2(r, c) - j).astype(F32)
        lg = _log_gamma_of(_head_of(r, c))
        decay_ref[...] = jnp.where(diff >= 0, jnp.exp(lg * jnp.maximum(diff, 0.0)), 0.0)
        r = lax.broadcasted_iota(jnp.int32, (rows, RET_DV), 0)
        xi_ref[...] = jnp.exp(_log_gamma_of(_head_of(r, c)) * (_mod_pow2(r, c).astype(F32) + 1.0))
        i = lax.broadcasted_iota(jnp.int32, (c, GROUP_W), 0)
        l = lax.broadcasted_iota(jnp.int32, (c, GROUP_W), 1)
        zeta_ref[...] = jnp.exp(_log_gamma_of(_head_of(l, HEAD_DIM)) * (c - 1.0 - i.astype(F32)))
        r = lax.broadcasted_iota(jnp.int32, (GROUP_W, RET_DV), 0)
        gc_ref[...] = jnp.exp(_log_gamma_of(_head_of(r, HEAD_DIM)) * float(c))

    state = state_ref[...]
    gc = gc_ref[...]
    for ci in range(RET_CHUNKS_PER_STEP):
        rsl = slice(ci * c, (ci + 1) * c)
        q, k, v = q_ref[rsl, :], k_ref[rsl, :], v_ref[rsl, :]
        qs = _stack_heads(q, c)
        scores = (_dot_nt(qs, k) * decay_ref[...]).astype(BF16)
        inner = jnp.concatenate(
            [_dot(scores[hh * c:(hh + 1) * c, :], v[:, hh * RET_DV:(hh + 1) * RET_DV]) for hh in range(N_HEADS)],
            axis=0)
        o = inner + _dot(qs, state.astype(BF16)) * xi_ref[...]
        mu = jnp.mean(o, axis=-1, keepdims=True)
        oc = o - mu
        var = jnp.mean(oc * oc, axis=-1, keepdims=True)
        on = oc * lax.rsqrt(var + GN_EPS)
        for hh in range(N_HEADS):
            lanes = slice(hh * RET_DV, (hh + 1) * RET_DV)
            o_ref[rsl, lanes] = (_silu(g_ref[rsl, lanes]) * on[hh * c:(hh + 1) * c, :]).astype(o_ref.dtype)
        kz_t = (k.astype(F32) * zeta_ref[...]).T.astype(BF16)
        kv = _dot(kz_t, v)
        state = jnp.concatenate(
            [state[hh * HEAD_DIM:(hh + 1) * HEAD_DIM, :] * gc[hh * HEAD_DIM:(hh + 1) * HEAD_DIM, :]
             + kv[hh * HEAD_DIM:(hh + 1) * HEAD_DIM, hh * RET_DV:(hh + 1) * RET_DV] for hh in range(N_HEADS)], axis=0)
    state_ref[...] = state


def _retention(rq, rk, rv, rg):
    s = rq.shape[0]
    c = RET_BLOCK
    step_rows = c * RET_CHUNKS_PER_STEP
    assert s % step_rows == 0

    def rows(width):
        return _rows(step_rows, width)

    return pl.pallas_call(
        _ret_kernel,
        out_shape=jax.ShapeDtypeStruct((s, RET_VW), BF16),
        grid=(s // step_rows,),
        in_specs=[rows(GROUP_W), rows(GROUP_W), rows(RET_VW), rows(RET_VW)],
        out_specs=rows(RET_VW),
        scratch_shapes=[pltpu.VMEM((GROUP_W, RET_DV), F32), pltpu.VMEM((N_HEADS * c, c), F32),
                        pltpu.VMEM((N_HEADS * c, RET_DV), F32), pltpu.VMEM((c, GROUP_W), F32),
                        pltpu.VMEM((GROUP_W, RET_DV), F32)],
        compiler_params=_params("arbitrary"),
        name="retention",
    )(rq, rk, rv, rg)


def _load_rows(ref, start, rows, stride):
    idx = pl.ds(start, rows) if stride == 1 else pl.ds(start, rows, stride=stride)
    return jnp.concatenate([ref[sl, idx, :] for sl in range(N_SLABS)], axis=1)


def _store_rows(ref, start, rows, stride, val):
    idx = pl.ds(start, rows) if stride == 1 else pl.ds(start, rows, stride=stride)
    for sl in range(N_SLABS):
        ref[sl, idx, :] = val[:, sl * V7X_LANES:(sl + 1) * V7X_LANES]


def _dil_kernel(q_ref, k_ref, v_ref, o_ref, kwin_ref, vwin_ref, part_o_ref, part_l_ref, const_ref):
    t, blk = DIL_TILE, DIL_BLOCK
    b = pl.program_id(0)

    @pl.when(b == 0)
    def _first():
        kwin_ref[:, 0:blk, :] = jnp.zeros((N_SLABS, blk, V7X_LANES), F32)
        vwin_ref[:, 0:blk, :] = jnp.zeros((N_SLABS, blk, V7X_LANES), F32)

    @pl.when(b > 0)
    def _shift():
        kwin_ref[:, 0:blk, :] = kwin_ref[:, blk:2 * blk, :]
        vwin_ref[:, 0:blk, :] = vwin_ref[:, blk:2 * blk, :]

    kwin_ref[:, blk:2 * blk, :] = k_ref[...]
    vwin_ref[:, blk:2 * blk, :] = v_ref[...]

    rows = N_HEADS * t
    qi = _mod_pow2(lax.broadcasted_iota(jnp.int32, (rows, 2 * t), 0), t)
    kc = lax.broadcasted_iota(jnp.int32, (rows, 2 * t), 1)
    ahead = kc - qi
    band = jnp.where(ahead >= 0, jnp.where(ahead <= t, 0.0, MASKED), MASKED)
    const_ref[0] = jnp.where(_own_head_lanes(t), 1.0, 0.0)
    const_ref[1] = band
    const_ref[2] = jnp.where(kc >= t, band, MASKED)

    for pi, dil in enumerate(DILATIONS):

        def unit(u, carry, pi=pi, dil=dil):
            res = _mod_pow2(u, dil)
            ct = lax.shift_right_logical(u, dil.bit_length() - 1)
            q_start = res + ct * (dil * t)
            k_start = blk + q_start - dil * t
            mask_id = jnp.where(jnp.logical_and(b == 0, ct == 0), 2, 1)
            qf = _load_rows(q_ref, q_start, t, dil)
            qs = (jnp.concatenate([qf] * N_HEADS, axis=0) * const_ref[0]).astype(BF16)
            kk = _load_rows(kwin_ref, k_start, 2 * t, dil).astype(BF16)
            vv = _load_rows(vwin_ref, k_start, 2 * t, dil).astype(BF16)
            z = _dot_nt(qs, kk) + const_ref[mask_id]
            m = jnp.max(z, axis=-1, keepdims=True)
            p = jnp.exp(z - m)
            l = jnp.sum(p, axis=-1, keepdims=True)
            o = _dot(p.astype(BF16), vv) / l
            lse = jnp.broadcast_to(m + jnp.log(l), (rows, GROUP_W))
            _store_rows(part_o_ref.at[pi], q_start, t, dil, _merge_heads(o, t))
            _store_rows(part_l_ref.at[pi], q_start, t, dil, _merge_heads(lse, t))
            return carry

        lax.fori_loop(0, blk // t, unit, 0, unroll=8)

    def combine(ci, carry):
        r0 = pl.multiple_of(ci * t, t)
        for sl in range(N_SLABS):
            ls = [part_l_ref[pi, sl, pl.ds(r0, t), :] for pi in range(len(DILATIONS))]
            top = functools.reduce(jnp.maximum, ls)
            ws = [jnp.exp(x - top) for x in ls]
            num = sum(w * part_o_ref[pi, sl, pl.ds(r0, t), :] for pi, w in enumerate(ws))
            o_ref[pl.ds(r0, t), sl * V7X_LANES:(sl + 1) * V7X_LANES] = (num / sum(ws)).astype(o_ref.dtype)
        return carry

    lax.fori_loop(0, blk // t, combine, 0)


def _dilated_attention(dq, dk, dv):
    s = dq.shape[1]
    blk = DIL_BLOCK
    assert s % blk == 0 and all(w == DIL_TILE * d for w, d in zip(WINDOWS, DILATIONS))
    assert 2 * DIL_TILE == GROUP_W
    slab = pl.BlockSpec((N_SLABS, blk, V7X_LANES), lambda i: (0, i, 0))
    n_pat = len(DILATIONS)
    return pl.pallas_call(
        _dil_kernel,
        out_shape=jax.ShapeDtypeStruct((s, GROUP_W), BF16),
        grid=(s // blk,),
        in_specs=[slab, slab, slab],
        out_specs=pl.BlockSpec((blk, GROUP_W), lambda i: (i, 0)),
        scratch_shapes=[pltpu.VMEM((N_SLABS, 2 * blk, V7X_LANES), F32), pltpu.VMEM((N_SLABS, 2 * blk, V7X_LANES), F32),
                        pltpu.VMEM((n_pat, N_SLABS, blk, V7X_LANES), F32),
                        pltpu.VMEM((n_pat, N_SLABS, blk, V7X_LANES), F32),
                        pltpu.VMEM((3, N_HEADS * DIL_TILE, 2 * DIL_TILE), F32)],
        compiler_params=_params("arbitrary"),
        name="dilated_attn",
    )(dq, dk, dv)


def _sb_kernel(q_ref, k_ref, v_ref, o_ref, acc_ref, carry_ref):
    t = SB_TILE
    rows = N_HEADS * t
    r = lax.broadcasted_iota(jnp.int32, (rows, t), 0)
    col = lax.broadcasted_iota(jnp.int32, (rows, t), 1)
    causal = _mod_pow2(r, t) > col
    kj = lax.broadcasted_iota(jnp.int32, (t, t), 0)
    ks = lax.broadcasted_iota(jnp.int32, (t, t), 1)
    later = jnp.where(kj > ks, 1.0, 0.0).astype(BF16)

    def visit(qi, qs, jj, carry, on_diagonal=False, live=None):
        k0 = pl.multiple_of(jnp.maximum(qi - jj, 0) * t, t)
        z = _dot_nt(qs, k_ref[pl.ds(k0, t), :])
        softplus = jnp.maximum(z, 0.0) + jnp.log(1.0 + jnp.exp(-jnp.abs(z)))
        log_keep = -softplus
        if on_diagonal:
            log_keep = jnp.where(causal, log_keep, 0.0)
        if live is not None:
            log_keep = log_keep * live
        within = _dot(log_keep.astype(BF16), later)
        a = jnp.exp((z - softplus) + within + carry)
        if on_diagonal:
            a = jnp.where(causal, a, 0.0)
        if live is not None:
            a = a * live
        return _dot(a.astype(BF16), v_ref[pl.ds(k0, t), :]), carry + jnp.sum(log_keep, axis=-1, keepdims=True)

    pending = []
    for bi in range(SB_TILES_PER_STEP):
        qi = pl.program_id(0) * SB_TILES_PER_STEP + bi
        qs = _stack_heads(q_ref[bi * t:(bi + 1) * t, :], t)
        out0, carry0 = visit(qi, qs, 0, jnp.zeros((rows, 1), F32), on_diagonal=True)
        out1, carry1 = visit(qi, qs, 1, carry0, live=jnp.where(qi >= 1, 1.0, 0.0))
        acc_ref[bi] = out0 + out1
        carry_ref[bi] = carry1
        pending.append((qi, qs, jnp.max(carry1)))

    for bi, (qi, qs, carry_max) in enumerate(pending):

        def unfinished(state, qi=qi):
            jj, carry_max = state
            return jnp.logical_and(jj <= qi, carry_max > SB_STOP)

        def sweep(state, bi=bi, qi=qi, qs=qs):
            jj, _ = state
            out, carry = visit(qi, qs, jj, carry_ref[bi])
            acc_ref[bi] += out
            carry_ref[bi] = carry
            return jj + 1, jnp.max(carry)

        lax.while_loop(unfinished, sweep, (jnp.int32(2), carry_max))
        o_ref[bi * t:(bi + 1) * t, :] = _merge_heads(acc_ref[bi], t).astype(o_ref.dtype)


def _stick_breaking(sq, sk, sv):
    s = sq.shape[0]
    t = SB_TILE
    step_rows = t * SB_TILES_PER_STEP
    assert s % step_rows == 0
    rows = _rows(step_rows, GROUP_W)
    return pl.pallas_call(
        _sb_kernel,
        out_shape=jax.ShapeDtypeStruct((s, GROUP_W), BF16),
        grid=(s // step_rows,),
        in_specs=[rows, _resident((s, GROUP_W)), _resident((s, GROUP_W))],
        out_specs=rows,
        scratch_shapes=[pltpu.VMEM((SB_TILES_PER_STEP, N_HEADS * t, GROUP_W), F32),
                        pltpu.VMEM((SB_TILES_PER_STEP, N_HEADS * t, 1), F32)],
        compiler_params=_params("parallel"),
        name="stickbreak_attn",
    )(sq, sk, sv)


def kernel(x, c, w_ada, b_ada, ln_gain, ln_bias, ffn1_w_gate, ffn1_w_up, ffn1_w_down,
           w_in, w_out, ffn2_w_gate, ffn2_w_up, ffn2_w_down):
    batch, s, d = x.shape
    depth = w_ada.shape[0]
    assert batch == 1 and c.shape == (1, d)
    alpha = (2.0 * depth) ** 0.25
    half = HEAD_DIM // 2
    ret_freq = 1.0 / (ROPE_THETA ** jnp.linspace(0.0, 1.0, half, dtype=F32))
    rope_freq = 1.0 / (ROPE_THETA ** (jnp.arange(0, HEAD_DIM, 2, dtype=F32) / HEAD_DIM))
    tables = _rope_tables(ret_freq, s) + _rope_tables(rope_freq, s)
    mod = _modulation(c, w_ada, b_ada)
    ln = jnp.stack([ln_gain, ln_bias], axis=2).reshape(depth, 6, d)
    ffn1_stacks = (ffn1_w_gate, ffn1_w_up, ffn1_w_down)
    rest_stacks = (ffn2_w_gate, ffn2_w_up, ffn2_w_down, w_in, w_out)
    ffn1 = [w[0].astype(BF16) for w in ffn1_stacks]
    xs = x[0]
    for l in range(depth):
        xs, rest = _ffn_sublayer(xs, l, mod, ln, ffn1, rest_stacks, alpha)
        ffn2, (win, wout) = rest[:3], rest[3:]
        nxt = ffn1_stacks if l + 1 < depth else ()
        projected = _in_projection(xs, l, mod, win, tables, nxt, l + 1)
        rq, rk, rv, rg, dq, dk, dv, sq, sk, sv = projected[:N_INPROJ_OUT]
        ffn1 = projected[N_INPROJ_OUT:]
        y_ret = _retention(rq, rk, rv, rg)
        y_dil = _dilated_attention(dq, dk, dv)
        y_sb = _stick_breaking(sq, sk, sv)
        xs = _mixout_ffn_sublayer(xs, y_ret, y_dil, y_sb, l, mod, ln, wout, ffn2, alpha)
    return xs[None]
```

```python
import functools
import math

import jax
import jax.numpy as jnp
from jax import lax
from jax.experimental import pallas as pl
from jax.experimental.pallas import tpu as pltpu

HEAD_DIM = 64
N_HEADS = 4
RET_DV = 128
GROUP_W = N_HEADS * HEAD_DIM
RET_VW = N_HEADS * RET_DV
WINDOWS = (128, 512, 2048)
DILATIONS = (1, 4, 16)
ROPE_THETA = 10000.0
LN_EPS = 1e-5
GN_EPS = 1e-6
FFN_RES = 0.5
N_MOD = 9
LOG_GAMMA = tuple(math.log1p(-(2.0 ** (-5 - h))) for h in range(N_HEADS))

V7X_LANES = 128
N_SLABS = GROUP_W // V7X_LANES
V7X_VMEM_LIMIT_BYTES = 56 * 1024 * 1024
MOD_COLS = 2304
ROW_TILE = 512
FFN_ROW_TILE = 1024
WEIGHT_CAST_STEPS = 16
FFN_CHUNKS = 11
RET_BLOCK = 256
RET_CHUNKS_PER_STEP = 8
DIL_TILE = 128
DIL_BLOCK = DIL_TILE * max(DILATIONS)
SB_TILE = 256
SB_TILES_PER_STEP = 8
MASKED = -1e30
SB_STOP = -110.0

F32 = jnp.float32
BF16 = jnp.bfloat16


def _params(*sem):
    return pltpu.CompilerParams(dimension_semantics=sem, vmem_limit_bytes=V7X_VMEM_LIMIT_BYTES)


def _resident(shape):
    return pl.BlockSpec(shape, lambda *_: (0,) * len(shape), pipeline_mode=pl.Buffered(1))


def _layer_block(tail, layer):
    return pl.BlockSpec((None,) + tuple(tail), lambda *_: (layer,) + (0,) * len(tail), pipeline_mode=pl.Buffered(1))


def _dot(a, b):
    return jnp.dot(a, b, preferred_element_type=F32)


def _dot_nt(a, b):
    return lax.dot_general(a, b, (((1,), (1,)), ((), ())), preferred_element_type=F32)


def _silu(v):
    return v * jax.nn.sigmoid(v)


def _post_norm(x, y, gate, res_w, gain, bias, alpha):
    z = alpha * x + (res_w * (1.0 + gate)) * y
    mu = jnp.mean(z, axis=-1, keepdims=True)
    zc = z - mu
    var = jnp.mean(zc * zc, axis=-1, keepdims=True)
    return zc * lax.rsqrt(var + LN_EPS) * gain + bias


def _mod_kernel(c_ref, w_ref, b_ref, o_ref):
    c = c_ref[...]
    o_ref[...] = jnp.sum(_silu(c) * w_ref[...], axis=0, keepdims=True) + b_ref[...]


def _modulation(c, w_ada, b_ada):
    depth, d, n = w_ada.shape
    tn = MOD_COLS
    assert n % tn == 0
    out = pl.pallas_call(
        _mod_kernel,
        out_shape=jax.ShapeDtypeStruct((depth, 1, n), F32),
        grid=(depth, n // tn),
        in_specs=[
            pl.BlockSpec((d, 1), lambda l, j: (0, 0)),
            pl.BlockSpec((None, d, tn), lambda l, j: (l, 0, j)),
            pl.BlockSpec((None, 1, tn), lambda l, j: (l, 0, j)),
        ],
        out_specs=pl.BlockSpec((None, 1, tn), lambda l, j: (l, 0, j)),
        compiler_params=_params("parallel", "parallel"),
        name="adaln_mod",
    )(c.reshape(d, 1), w_ada, b_ada.reshape(depth, 1, n))
    return out.reshape(depth, N_MOD, d)


def _mod_rows(mod_ref, sub):
    return tuple(mod_ref[3 * sub + i:3 * sub + i + 1, :] for i in range(3))


def _ln_rows(ln_ref, sub):
    return ln_ref[2 * sub:2 * sub + 1, :], ln_ref[2 * sub + 1:2 * sub + 2, :]


def _ffn_postnorm(x, sub, mod_ref, ln_ref, wg_ref, wu_ref, wd_ref, alpha):
    shift, scale, gate = _mod_rows(mod_ref, sub)
    d_ff = wg_ref.shape[1]
    fc = d_ff // FFN_CHUNKS
    h = (x * (1.0 + scale) + shift).astype(BF16)
    y = jnp.zeros(x.shape, F32)
    for ci in range(FFN_CHUNKS):
        cols = slice(ci * fc, (ci + 1) * fc)
        act = _silu(_dot(h, wg_ref[:, cols])) * _dot(h, wu_ref[:, cols])
        y = y + _dot(act.astype(BF16), wd_ref[cols, :])
    return _post_norm(x, y, gate, FFN_RES, *_ln_rows(ln_ref, sub), alpha)


def _cast_specs(stack, layer, grid_steps):
    _, r, c = stack.shape
    steps = min(grid_steps, WEIGHT_CAST_STEPS)
    chunk = r // steps
    assert r % steps == 0 and chunk % 16 == 0
    last = steps - 1
    return (pl.BlockSpec((None, chunk, c), lambda i: (layer, jnp.minimum(i, last), 0)),
            pl.BlockSpec((chunk, c), lambda i: (jnp.minimum(i, last), 0)),
            jax.ShapeDtypeStruct((r, c), BF16))


def _cast_chunks(src_refs, dst_refs):
    for src, dst in zip(src_refs, dst_refs):
        dst[...] = src[...].astype(dst.dtype)


def _ffn_kernel(x_ref, mod_ref, ln_ref, wg_ref, wu_ref, wd_ref, *refs, alpha):
    n_cast = len(refs) // 2
    o_ref = refs[n_cast]
    o_ref[...] = _ffn_postnorm(x_ref[...], 0, mod_ref, ln_ref, wg_ref, wu_ref, wd_ref, alpha)
    _cast_chunks(refs[:n_cast], refs[n_cast + 1:])


def _mixout_ffn_kernel(x_ref, yr_ref, yd_ref, ys_ref, mod_ref, ln_ref, wo_ref, wg_ref, wu_ref, wd_ref, o_ref,
                       *, alpha):
    vw, g = RET_VW, GROUP_W
    y = (_dot(yr_ref[...], wo_ref[0:vw, :]) + _dot(yd_ref[...], wo_ref[vw:vw + g, :])
         + _dot(ys_ref[...], wo_ref[vw + g:vw + 2 * g, :]))
    x1 = _post_norm(x_ref[...], y, _mod_rows(mod_ref, 1)[2], 1.0, *_ln_rows(ln_ref, 1), alpha)
    o_ref[...] = _ffn_postnorm(x1, 2, mod_ref, ln_ref, wg_ref, wu_ref, wd_ref, alpha)


def _ffn_specs(d, d_ff):
    assert d_ff % (FFN_CHUNKS * V7X_LANES) == 0
    return [_resident((d, d_ff)), _resident((d, d_ff)), _resident((d_ff, d))]


def _rows(tile, width):
    return pl.BlockSpec((tile, width), lambda i: (i, 0))


def _ffn_sublayer(x, layer, mod, ln, weights, cast_stacks, alpha):
    s, d = x.shape
    assert s % FFN_ROW_TILE == 0
    casts = [_cast_specs(w, layer, s // FFN_ROW_TILE) for w in cast_stacks]
    outs = pl.pallas_call(
        functools.partial(_ffn_kernel, alpha=alpha),
        out_shape=[jax.ShapeDtypeStruct((s, d), F32)] + [c[2] for c in casts],
        grid=(s // FFN_ROW_TILE,),
        in_specs=[_rows(FFN_ROW_TILE, d), _layer_block((N_MOD, d), layer), _layer_block((6, d), layer)]
        + _ffn_specs(d, weights[0].shape[1]) + [c[0] for c in casts],
        out_specs=[_rows(FFN_ROW_TILE, d)] + [c[1] for c in casts],
        compiler_params=_params("parallel"),
        name="ffn_postnorm",
    )(x, mod, ln, *weights, *cast_stacks)
    return outs[0], outs[1:]


def _mixout_ffn_sublayer(x, y_ret, y_dil, y_sb, layer, mod, ln, w_out, weights, alpha):
    s, d = x.shape
    assert s % FFN_ROW_TILE == 0
    t = FFN_ROW_TILE
    return pl.pallas_call(
        functools.partial(_mixout_ffn_kernel, alpha=alpha),
        out_shape=jax.ShapeDtypeStruct((s, d), F32),
        grid=(s // t,),
        in_specs=[_rows(t, d), _rows(t, RET_VW), _rows(t, GROUP_W), _rows(t, GROUP_W),
                  _layer_block((N_MOD, d), layer), _layer_block((6, d), layer), _resident(w_out.shape)]
        + _ffn_specs(d, weights[0].shape[1]),
        out_specs=_rows(t, d),
        compiler_params=_params("parallel"),
        name="mixout_ffn_postnorm",
    )(x, y_ret, y_dil, y_sb, mod, ln, w_out, *weights)


def _rope(v, cos, sin):
    lane = lax.broadcasted_iota(jnp.int32, cos.shape, 1)
    first_half = (lane & (HEAD_DIM - 1)) < (HEAD_DIM // 2)
    halves = []
    for hf in range(GROUP_W // V7X_LANES):
        vh = v[:, hf * V7X_LANES:(hf + 1) * V7X_LANES]
        partner = jnp.where(first_half,
                            pltpu.roll(vh, V7X_LANES - HEAD_DIM // 2, 1),
                            pltpu.roll(vh, HEAD_DIM // 2, 1))
        halves.append(vh * cos + partner * sin)
    return jnp.concatenate(halves, axis=1)


def _tile_rope(within_ref, start_ref):
    cos_r, sin_r = within_ref[0], within_ref[1]
    c0, s0, c0s, s0s = (start_ref[i:i + 1, :] for i in range(4))
    return c0 * cos_r - s0 * sin_r, s0s * cos_r + c0s * sin_r


N_INPROJ_IN, N_INPROJ_OUT = 7, 10


def _inproj_kernel(*refs):
    n_cast = (len(refs) - N_INPROJ_IN - N_INPROJ_OUT) // 2
    x_ref, mod_ref, w_ref, rw_ref, rs_ref, dw_ref, ds_ref = refs[:N_INPROJ_IN]
    out_refs = refs[N_INPROJ_IN + n_cast:N_INPROJ_IN + n_cast + N_INPROJ_OUT]
    rq_ref, rk_ref, rv_ref, rg_ref, dq_ref, dk_ref, dv_ref, sq_ref, sk_ref, sv_ref = out_refs
    _cast_chunks(refs[N_INPROJ_IN:N_INPROJ_IN + n_cast], refs[N_INPROJ_IN + n_cast + N_INPROJ_OUT:])
    shift, scale, _ = _mod_rows(mod_ref, 1)
    h = (x_ref[...] * (1.0 + scale) + shift).astype(BF16)
    qk_scale = HEAD_DIM ** -0.5
    g, vw = GROUP_W, RET_VW

    full = _dot(h, w_ref[...])

    def proj(start, width):
        return full[:, start:start + width]

    cr, sr = _tile_rope(rw_ref, rs_ref)
    cd, sd = _tile_rope(dw_ref, ds_ref)
    rq_ref[...] = _rope(proj(0, g), cr, sr).astype(BF16)
    rk_ref[...] = (_rope(proj(g, g), cr, sr) * qk_scale).astype(BF16)
    rv_ref[...] = proj(2 * g, vw).astype(BF16)
    rg_ref[...] = proj(2 * g + vw, vw)
    base = 2 * g + 2 * vw
    for ref, val in ((dq_ref, _rope(proj(base, g), cd, sd) * qk_scale),
                     (dk_ref, _rope(proj(base + g, g), cd, sd)), (dv_ref, proj(base + 2 * g, g))):
        for sl in range(N_SLABS):
            ref[sl] = val[:, sl * V7X_LANES:(sl + 1) * V7X_LANES]
    sq_ref[...] = (proj(base + 3 * g, g) * qk_scale).astype(BF16)
    sk_ref[...] = proj(base + 4 * g, g).astype(BF16)
    sv_ref[...] = proj(base + 5 * g, g).astype(BF16)


def _in_projection(x, layer, mod, w_in, tables, cast_stacks, cast_layer):
    s, d = x.shape
    n = w_in.shape[1]
    g, vw = GROUP_W, RET_VW
    assert n == 8 * g + 2 * vw

    def rows(width):
        return _rows(ROW_TILE, width)

    def flat(width, dtype):
        return jax.ShapeDtypeStruct((s, width), dtype), rows(width)

    slabs = (jax.ShapeDtypeStruct((N_SLABS, s, V7X_LANES), F32),
             pl.BlockSpec((N_SLABS, ROW_TILE, V7X_LANES), lambda i: (0, i, 0)))
    outs = (flat(g, BF16), flat(g, BF16), flat(vw, BF16), flat(vw, F32), slabs, slabs, slabs,
            flat(g, BF16), flat(g, BF16), flat(g, BF16))
    assert len(outs) == N_INPROJ_OUT
    casts = [_cast_specs(w, cast_layer, s // ROW_TILE) for w in cast_stacks]
    return pl.pallas_call(
        _inproj_kernel,
        out_shape=[o[0] for o in outs] + [c[2] for c in casts],
        grid=(s // ROW_TILE,),
        in_specs=[rows(d), _layer_block((N_MOD, d), layer), _resident((d, n))]
        + [_resident((2, ROW_TILE, V7X_LANES)), pl.BlockSpec((None, 4, V7X_LANES), lambda i: (i, 0, 0))] * 2
        + [c[0] for c in casts],
        out_specs=[o[1] for o in outs] + [c[1] for c in casts],
        compiler_params=_params("parallel"),
        name="mixer_inproj",
    )(x, mod, w_in, *tables, *cast_stacks)


def _rope_tables(inv_freq, s):
    half = HEAD_DIM // 2
    lane_freq = jnp.tile(inv_freq, V7X_LANES // half)[None, :]
    sign = jnp.tile(jnp.concatenate([-jnp.ones(half, F32), jnp.ones(half, F32)]), V7X_LANES // HEAD_DIM)[None, :]
    within = jnp.arange(ROW_TILE, dtype=F32)[:, None] * lane_freq
    start = (jnp.arange(s // ROW_TILE, dtype=F32) * ROW_TILE)[:, None] * lane_freq
    c0, s0 = jnp.cos(start), jnp.sin(start)
    return (jnp.stack([jnp.cos(within), jnp.sin(within)]), jnp.stack([c0, s0, sign * c0, sign * s0], axis=1))


def _head_of(idx, width):
    assert width & (width - 1) == 0
    return lax.shift_right_logical(idx, width.bit_length() - 1)


def _mod_pow2(idx, width):
    assert width & (width - 1) == 0
    return idx & (width - 1)


def _log_gamma_of(head):
    lg = jnp.full(head.shape, LOG_GAMMA[N_HEADS - 1], F32)
    for hh in range(N_HEADS - 2, -1, -1):
        lg = jnp.where(head == hh, LOG_GAMMA[hh], lg)
    return lg


def _own_head_lanes(rows):
    r = lax.broadcasted_iota(jnp.int32, (N_HEADS * rows, GROUP_W), 0)
    l = lax.broadcasted_iota(jnp.int32, (N_HEADS * rows, GROUP_W), 1)
    return _head_of(r, rows) == _head_of(l, HEAD_DIM)


def _stack_heads(q, rows):
    qf = q.astype(F32)
    return jnp.where(_own_head_lanes(rows), jnp.concatenate([qf] * N_HEADS, axis=0), 0.0).astype(q.dtype)


def _merge_heads(stacked, rows):
    l = lax.broadcasted_iota(jnp.int32, (rows, GROUP_W), 1)
    head = _head_of(l, HEAD_DIM)
    out = jnp.zeros((rows, GROUP_W), stacked.dtype)
    for hh in range(N_HEADS):
        out = jnp.where(head == hh, stacked[hh * rows:(hh + 1) * rows, :], out)
    return out


def _ret_kernel(q_ref, k_ref, v_ref, g_ref, o_ref, state_ref, decay_ref, xi_ref, zeta_ref, gc_ref):
    c = RET_BLOCK
    rows = N_HEADS * c

    @pl.when(pl.program_id(0) == 0)
    def _init():
        state_ref[...] = jnp.zeros(state_ref.shape, F32)
        r = lax.broadcasted_iota(jnp.int32, (rows, c), 0)
        j = lax.broadcasted_iota(jnp.int32, (rows, c), 1)
        diff = (_mod_pow2(r, c) - j).astype(F32)
        lg = _log_gamma_of(_head_of(r, c))
        decay_ref[...] = jnp.where(diff >= 0, jnp.exp(lg * jnp.maximum(diff, 0.0)), 0.0)
        r = lax.broadcasted_iota(jnp.int32, (rows, RET_DV), 0)
        xi_ref[...] = jnp.exp(_log_gamma_of(_head_of(r, c)) * (_mod_pow2(r, c).astype(F32) + 1.0))
        i = lax.broadcasted_iota(jnp.int32, (c, GROUP_W), 0)
        l = lax.broadcasted_iota(jnp.int32, (c, GROUP_W), 1)
        zeta_ref[...] = jnp.exp(_log_gamma_of(_head_of(l, HEAD_DIM)) * (c - 1.0 - i.astype(F32)))
        r = lax.broadcasted_iota(jnp.int32, (GROUP_W, RET_DV), 0)
        gc_ref[...] = jnp.exp(_log_gamma_of(_head_of(r, HEAD_DIM)) * float(c))

    state = state_ref[...]
    gc = gc_ref[...]
    for ci in range(RET_CHUNKS_PER_STEP):
        rsl = slice(ci * c, (ci + 1) * c)
        q, k, v = q_ref[rsl, :], k_ref[rsl, :], v_ref[rsl, :]
        qs = _stack_heads(q, c)
        scores = (_dot_nt(qs, k) * decay_ref[...]).astype(BF16)
        inner = jnp.concatenate(
            [_dot(scores[hh * c:(hh + 1) * c, :], v[:, hh * RET_DV:(hh + 1) * RET_DV]) for hh in range(N_HEADS)],
            axis=0)
        o = inner + _dot(qs, state.astype(BF16)) * xi_ref[...]
        mu = jnp.mean(o, axis=-1, keepdims=True)
        oc = o - mu
        var = jnp.mean(oc * oc, axis=-1, keepdims=True)
        on = oc * lax.rsqrt(var + GN_EPS)
        for hh in range(N_HEADS):
            lanes = slice(hh * RET_DV, (hh + 1) * RET_DV)
            o_ref[rsl, lanes] = (_silu(g_ref[rsl, lanes]) * on[hh * c:(hh + 1) * c, :]).astype(o_ref.dtype)
        kz_t = (k.astype(F32) * zeta_ref[...]).T.astype(BF16)
        kv = _dot(kz_t, v)
        state = jnp.concatenate(
            [state[hh * HEAD_DIM:(hh + 1) * HEAD_DIM, :] * gc[hh * HEAD_DIM:(hh + 1) * HEAD_DIM, :]
             + kv[hh * HEAD_DIM:(hh + 1) * HEAD_DIM, hh * RET_DV:(hh + 1) * RET_DV] for hh in range(N_HEADS)], axis=0)
    state_ref[...] = state


def _retention(rq, rk, rv, rg):
    s = rq.shape[0]
    c = RET_BLOCK
    step_rows = c * RET_CHUNKS_PER_STEP
    assert s % step_rows == 0

    def rows(width):
        return _rows(step_rows, width)

    return pl.pallas_call(
        _ret_kernel,
        out_shape=jax.ShapeDtypeStruct((s, RET_VW), BF16),
        grid=(s // step_rows,),
        in_specs=[rows(GROUP_W), rows(GROUP_W), rows(RET_VW), rows(RET_VW)],
        out_specs=rows(RET_VW),
        scratch_shapes=[pltpu.VMEM((GROUP_W, RET_DV), F32), pltpu.VMEM((N_HEADS * c, c), F32),
                        pltpu.VMEM((N_HEADS * c, RET_DV), F32), pltpu.VMEM((c, GROUP_W), F32),
                        pltpu.VMEM((GROUP_W, RET_DV), F32)],
        compiler_params=_params("arbitrary"),
        name="retention",
    )(rq, rk, rv, rg)


def _load_rows(ref, start, rows, stride):
    idx = pl.ds(start, rows) if stride == 1 else pl.ds(start, rows, stride=stride)
    return jnp.concatenate([ref[sl, idx, :] for sl in range(N_SLABS)], axis=1)


def _store_rows(ref, start, rows, stride, val):
    idx = pl.ds(start, rows) if stride == 1 else pl.ds(start, rows, stride=stride)
    for sl in range(N_SLABS):
        ref[sl, idx, :] = val[:, sl * V7X_LANES:(sl + 1) * V7X_LANES]


def _dil_kernel(q_ref, k_ref, v_ref, o_ref, kwin_ref, vwin_ref, part_o_ref, part_l_ref, const_ref):
    t, blk = DIL_TILE, DIL_BLOCK
    b = pl.program_id(0)

    @pl.when(b == 0)
    def _first():
        kwin_ref[:, 0:blk, :] = jnp.zeros((N_SLABS, blk, V7X_LANES), F32)
        vwin_ref[:, 0:blk, :] = jnp.zeros((N_SLABS, blk, V7X_LANES), F32)

    @pl.when(b > 0)
    def _shift():
        kwin_ref[:, 0:blk, :] = kwin_ref[:, blk:2 * blk, :]
        vwin_ref[:, 0:blk, :] = vwin_ref[:, blk:2 * blk, :]

    kwin_ref[:, blk:2 * blk, :] = k_ref[...]
    vwin_ref[:, blk:2 * blk, :] = v_ref[...]

    rows = N_HEADS * t
    qi = _mod_pow2(lax.broadcasted_iota(jnp.int32, (rows, 2 * t), 0), t)
    kc = lax.broadcasted_iota(jnp.int32, (rows, 2 * t), 1)
    ahead = kc - qi
    band = jnp.where(ahead >= 0, jnp.where(ahead <= t, 0.0, MASKED), MASKED)
    const_ref[0] = jnp.where(_own_head_lanes(t), 1.0, 0.0)
    const_ref[1] = band
    const_ref[2] = jnp.where(kc >= t, band, MASKED)

    for pi, dil in enumerate(DILATIONS):

        def unit(u, carry, pi=pi, dil=dil):
            res = _mod_pow2(u, dil)
            ct = lax.shift_right_logical(u, dil.bit_length() - 1)
            q_start = res + ct * (dil * t)
            k_start = blk + q_start - dil * t
            mask_id = jnp.where(jnp.logical_and(b == 0, ct == 0), 2, 1)
            qf = _load_rows(q_ref, q_start, t, dil)
            qs = (jnp.concatenate([qf] * N_HEADS, axis=0) * const_ref[0]).astype(BF16)
            kk = _load_rows(kwin_ref, k_start, 2 * t, dil).astype(BF16)
            vv = _load_rows(vwin_ref, k_start, 2 * t, dil).astype(BF16)
            z = _dot_nt(qs, kk) + const_ref[mask_id]
            m = jnp.max(z, axis=-1, keepdims=True)
            p = jnp.exp(z - m)
            l = jnp.sum(p, axis=-1, keepdims=True)
            o = _dot(p.astype(BF16), vv) / l
            lse = jnp.broadcast_to(m + jnp.log(l), (rows, GROUP_W))
            _store_rows(part_o_ref.at[pi], q_start, t, dil, _merge_heads(o, t))
            _store_rows(part_l_ref.at[pi], q_start, t, dil, _merge_heads(lse, t))
            return carry

        lax.fori_loop(0, blk // t, unit, 0, unroll=16)

    def combine(ci, carry):
        r0 = pl.multiple_of(ci * t, t)
        for sl in range(N_SLABS):
            ls = [part_l_ref[pi, sl, pl.ds(r0, t), :] for pi in range(len(DILATIONS))]
            top = functools.reduce(jnp.maximum, ls)
            ws = [jnp.exp(x - top) for x in ls]
            num = sum(w * part_o_ref[pi, sl, pl.ds(r0, t), :] for pi, w in enumerate(ws))
            o_ref[pl.ds(r0, t), sl * V7X_LANES:(sl + 1) * V7X_LANES] = (num / sum(ws)).astype(o_ref.dtype)
        return carry

    lax.fori_loop(0, blk // t, combine, 0)


def _dilated_attention(dq, dk, dv):
    s = dq.shape[1]
    blk = DIL_BLOCK
    assert s % blk == 0 and all(w == DIL_TILE * d for w, d in zip(WINDOWS, DILATIONS))
    assert 2 * DIL_TILE == GROUP_W
    slab = pl.BlockSpec((N_SLABS, blk, V7X_LANES), lambda i: (0, i, 0))
    n_pat = len(DILATIONS)
    return pl.pallas_call(
        _dil_kernel,
        out_shape=jax.ShapeDtypeStruct((s, GROUP_W), BF16),
        grid=(s // blk,),
        in_specs=[slab, slab, slab],
        out_specs=pl.BlockSpec((blk, GROUP_W), lambda i: (i, 0)),
        scratch_shapes=[pltpu.VMEM((N_SLABS, 2 * blk, V7X_LANES), F32), pltpu.VMEM((N_SLABS, 2 * blk, V7X_LANES), F32),
                        pltpu.VMEM((n_pat, N_SLABS, blk, V7X_LANES), F32),
                        pltpu.VMEM((n_pat, N_SLABS, blk, V7X_LANES), F32),
                        pltpu.VMEM((3, N_HEADS * DIL_TILE, 2 * DIL_TILE), F32)],
        compiler_params=_params("arbitrary"),
        name="dilated_attn",
    )(dq, dk, dv)


def _sb_kernel(q_ref, k_ref, v_ref, o_ref, acc_ref, carry_ref):
    t = SB_TILE
    rows = N_HEADS * t
    r = lax.broadcasted_iota(jnp.int32, (rows, t), 0)
    col = lax.broadcasted_iota(jnp.int32, (rows, t), 1)
    causal = _mod_pow2(r, t) > col
    kj = lax.broadcasted_iota(jnp.int32, (t, t), 0)
    ks = lax.broadcasted_iota(jnp.int32, (t, t), 1)
    later = jnp.where(kj > ks, 1.0, 0.0).astype(BF16)

    def visit(qi, qs, jj, carry, on_diagonal=False, live=None):
        k0 = pl.multiple_of(jnp.maximum(qi - jj, 0) * t, t)
        z = _dot_nt(qs, k_ref[pl.ds(k0, t), :])
        softplus = jnp.maximum(z, 0.0) + jnp.log(1.0 + jnp.exp(-jnp.abs(z)))
        log_keep = -softplus
        if on_diagonal:
            log_keep = jnp.where(causal, log_keep, 0.0)
        if live is not None:
            log_keep = log_keep * live
        within = _dot(log_keep.astype(BF16), later)
        a = jnp.exp((z - softplus) + within + carry)
        if on_diagonal:
            a = jnp.where(causal, a, 0.0)
        if live is not None:
            a = a * live
        return _dot(a.astype(BF16), v_ref[pl.ds(k0, t), :]), carry + jnp.sum(log_keep, axis=-1, keepdims=True)

    pending = []
    for bi in range(SB_TILES_PER_STEP):
        qi = pl.program_id(0) * SB_TILES_PER_STEP + bi
        qs = _stack_heads(q_ref[bi * t:(bi + 1) * t, :], t)
        out0, carry0 = visit(qi, qs, 0, jnp.zeros((rows, 1), F32), on_diagonal=True)
        out1, carry1 = visit(qi, qs, 1, carry0, live=jnp.where(qi >= 1, 1.0, 0.0))
        acc_ref[bi] = out0 + out1
        carry_ref[bi] = carry1
        pending.append((qi, qs, jnp.max(carry1)))

    for bi, (qi, qs, carry_max) in enumerate(pending):

        def unfinished(state, qi=qi):
            jj, carry_max = state
            return jnp.logical_and(jj <= qi, carry_max > SB_STOP)

        def sweep(state, bi=bi, qi=qi, qs=qs):
            jj, _ = state
            out, carry = visit(qi, qs, jj, carry_ref[bi])
            acc_ref[bi] += out
            carry_ref[bi] = carry
            return jj + 1, jnp.max(carry)

        lax.while_loop(unfinished, sweep, (jnp.int32(2), carry_max))
        o_ref[bi * t:(bi + 1) * t, :] = _merge_heads(acc_ref[bi], t).astype(o_ref.dtype)


def _stick_breaking(sq, sk, sv):
    s = sq.shape[0]
    t = SB_TILE
    step_rows = t * SB_TILES_PER_STEP
    assert s % step_rows == 0
    rows = _rows(step_rows, GROUP_W)
    return pl.pallas_call(
        _sb_kernel,
        out_shape=jax.ShapeDtypeStruct((s, GROUP_W), BF16),
        grid=(s // step_rows,),
        in_specs=[rows, _resident((s, GROUP_W)), _resident((s, GROUP_W))],
        out_specs=rows,
        scratch_shapes=[pltpu.VMEM((SB_TILES_PER_STEP, N_HEADS * t, GROUP_W), F32),
                        pltpu.VMEM((SB_TILES_PER_STEP, N_HEADS * t, 1), F32)],
        compiler_params=_params("parallel"),
        name="stickbreak_attn",
    )(sq, sk, sv)


def kernel(x, c, w_ada, b_ada, ln_gain, ln_bias, ffn1_w_gate, ffn1_w_up, ffn1_w_down,
           w_in, w_out, ffn2_w_gate, ffn2_w_up, ffn2_w_down):
    batch, s, d = x.shape
    depth = w_ada.shape[0]
    assert batch == 1 and c.shape == (1, d)
    alpha = (2.0 * depth) ** 0.25
    half = HEAD_DIM // 2
    ret_freq = 1.0 / (ROPE_THETA ** jnp.linspace(0.0, 1.0, half, dtype=F32))
    rope_freq = 1.0 / (ROPE_THETA ** (jnp.arange(0, HEAD_DIM, 2, dtype=F32) / HEAD_DIM))
    tables = _rope_tables(ret_freq, s) + _rope_tables(rope_freq, s)
    mod = _modulation(c, w_ada, b_ada)
    ln = jnp.stack([ln_gain, ln_bias], axis=2).reshape(depth, 6, d)
    ffn1_stacks = (ffn1_w_gate, ffn1_w_up, ffn1_w_down)
    rest_stacks = (ffn2_w_gate, ffn2_w_up, ffn2_w_down, w_in, w_out)
    ffn1 = [w[0].astype(BF16) for w in ffn1_stacks]
    xs = x[0]
    for l in range(depth):
        xs, rest = _ffn_sublayer(xs, l, mod, ln, ffn1, rest_stacks, alpha)
        ffn2, (win, wout) = rest[:3], rest[3:]
        nxt = ffn1_stacks if l + 1 < depth else ()
        projected = _in_projection(xs, l, mod, win, tables, nxt, l + 1)
        rq, rk, rv, rg, dq, dk, dv, sq, sk, sv = projected[:N_INPROJ_OUT]
        ffn1 = projected[N_INPROJ_OUT:]
        y_ret = _retention(rq, rk, rv, rg)
        y_dil = _dilated_attention(dq, dk, dv)
        y_sb = _stick_breaking(sq, sk, sv)
        xs = _mixout_ffn_sublayer(xs, y_ret, y_dil, y_sb, l, mod, ln, wout, ffn2, alpha)
    return xs[None]
```
